```python
import jax, jax.numpy as jnp
from jax import lax
import numpy as np

D_MODEL = 2048
BATCH = 16
SEQ = 256
DEPTH = 2
DEC_BATCH = 8
DEC_SEQ = 1024
PAST_LEN = 512

GRID_W = 64
A_WIDTH = 1024
A_HEADS = 8
A_DK = A_WIDTH // A_HEADS
A_DV = A_WIDTH // A_HEADS
B_WIDTH = 1024
B_GROUPS = 4
B_CHUNK = 128
C_GROUPS = 4
SCAN_CHUNK = 64
D_FF = 5632
CONV_W = 3
N_HGRN_LAYERS = (DEPTH + 1) // 2
IN_WIDTH_0 = 5 * A_WIDTH + 2 * B_WIDTH
IN_SPLITS = (A_WIDTH, 2 * A_WIDTH, 3 * A_WIDTH, 4 * A_WIDTH, 5 * A_WIDTH, 5 * A_WIDTH + B_WIDTH)
EPS = 1e-6

kernel_name = 'hybrid_hgrn2_gmlp_fnet_convffn_diffusion_step'


def _rmsnorm(x, g):
    xf = x.astype(jnp.float32)
    y = xf * lax.rsqrt(jnp.mean(xf * xf, axis=-1, keepdims=True) + EPS)
    return (y * g.astype(jnp.float32)).astype(x.dtype)


def _gla_scan(q, k, v, logf, s0):
    bsz, n, h, _ = q.shape
    dv = v.shape[-1]
    nc = n // SCAN_CHUNK

    def blocks(t):
        return t.reshape(bsz, nc, SCAN_CHUNK, h, t.shape[-1]).transpose(1, 0, 3, 2, 4)

    q, k, v, logf = blocks(q), blocks(k), blocks(v), blocks(logf)
    b = jnp.cumsum(logf, axis=3)
    mid = SCAN_CHUNK // 2
    ref = b[:, :, :, mid - 1:mid, :]
    b_last = b[:, :, :, -1:, :]
    scores = jnp.einsum('nbhik,nbhjk->nbhij', q * jnp.exp(b - ref), k * jnp.exp(ref - b))
    lower = jnp.tril(jnp.ones((SCAN_CHUNK, SCAN_CHUNK), dtype=bool))
    scores = jnp.where(lower, scores, 0.0)
    o_intra = jnp.einsum('nbhij,nbhjv->nbhiv', scores, v)
    q_in = q * jnp.exp(b)
    k_out = k * jnp.exp(b_last - b)
    decay = jnp.exp(b_last[:, :, :, 0, :])

    def step(S, xs):
        qc, kc, vc, dc = xs
        o = jnp.einsum('bhik,bhkv->bhiv', qc, S)
        S = dc[..., None] * S + jnp.einsum('bhjk,bhjv->bhkv', kc, vc)
        return S, o

    s_final, o_inter = lax.scan(step, s0, (q_in, k_out, v, decay))
    o = (o_intra + o_inter).transpose(1, 0, 3, 2, 4).reshape(bsz, n, h, dv)
    return o, s_final


def _mixers_ab(h, w_in, lb_f, lb_b, gnorm, vnorm, ws, bs, w_out, s0):
    bsz, n, _ = h.shape
    f32 = jnp.float32
    proj = h @ w_in
    qa, fzf, fzb, ia, ga, ub, vb = jnp.split(proj, IN_SPLITS, axis=-1)

    def heads(t):
        return t.astype(f32).reshape(bsz, n, A_HEADS, -1)

    q = heads(jax.nn.silu(qa)) * (A_DK ** -0.5)
    v = heads(ia)

    def gates(fz, lb):
        lbh = lb.reshape(A_HEADS, A_DK)
        f = lbh + (1.0 - lbh) * jax.nn.sigmoid(heads(fz))
        return 1.0 - f, jnp.log(f)

    k_f, lf_f = gates(fzf, lb_f)
    k_b, lf_b = gates(fzb, lb_b)
    s0 = s0.astype(f32)
    o_fwd, s_fwd = _gla_scan(q, k_f, v, lf_f, s0[:, 0])
    o_rev, s_bwd = _gla_scan(jnp.flip(q, 1), jnp.flip(k_b, 1), jnp.flip(v, 1), jnp.flip(lf_b, 1), s0[:, 1])
    o_rec = _rmsnorm(o_fwd + jnp.flip(o_rev, 1), gnorm)
    out_a = (o_rec.reshape(bsz, n, A_WIDTH) * jax.nn.silu(ga.astype(f32))).astype(h.dtype)

    u = jax.nn.gelu(ub)
    vv = jax.nn.gelu(vb)
    nck = n // B_CHUNK
    cg = B_WIDTH // B_GROUPS
    vv = _rmsnorm(vv.reshape(bsz, nck, B_CHUNK, B_GROUPS, cg), vnorm.reshape(B_GROUPS, cg))
    mixed = jnp.einsum('gpq,bnqgc->bnpgc', ws, vv) + bs.T[:, :, None]
    out_b = u * mixed.reshape(bsz, n, B_WIDTH)

    out = jnp.concatenate([out_a, out_b], axis=-1) @ w_out
    return out, jnp.stack([s_fwd, s_bwd], axis=1)


def _fourier_mix(h):
    bsz, n, d = h.shape
    hf = h.astype(jnp.float32).reshape(bsz, n, C_GROUPS, d // C_GROUPS)
    return jnp.fft.fftn(hf, axes=(1, 3), norm='ortho').real.reshape(bsz, n, d).astype(h.dtype)


def _conv_ffn(h, w_up, cw, cb, w_down, n_rows):
    bsz, n, _ = h.shape
    up = (h @ w_up).reshape(bsz, n_rows, n // n_rows, 2 * D_FF)
    pad = jnp.pad(up, ((0, 0), (0, 0), (1, 1), (0, 0)))
    conv = pad[:, :, :-2] * cw[0] + pad[:, :, 1:-1] * cw[1] + pad[:, :, 2:] * cw[2] + cb
    gate, val = jnp.split(conv.reshape(bsz, n, 2 * D_FF), 2, axis=-1)
    return (jax.nn.silu(gate) * val) @ w_down


def setup_inputs(seed: int = 0) -> dict:
    key = jax.random.key(seed)
    ks = iter(jax.random.split(key, 48))

    def nrm(shape, scale):
        return scale * jax.random.normal(next(ks), shape, jnp.float32)

    def gain(shape):
        return 1.0 + 0.01 * jax.random.normal(next(ks), shape, jnp.float32)

    d = D_MODEL
    inp = {}
    inp['x_prompt'] = nrm((BATCH, SEQ, d), 1.0)
    inp['x_sample'] = nrm((DEC_BATCH, DEC_SEQ, d), 1.0)
    inp['state_l0_hgrn'] = nrm((DEC_BATCH, 2, A_HEADS, A_DK, A_DV), 0.5)
    inp['c'] = nrm((DEC_BATCH, d), 1.0)
    inp['c_ctx'] = nrm((d,), 1.0)
    inp['mod_w_0'] = nrm((d, 6 * d), 0.5 * d ** -0.5)
    inp['mod_b_0'] = nrm((6 * d,), 0.01)
    inp['norm1_0'] = gain((d,))
    inp['w_in_0'] = nrm((d, IN_WIDTH_0), d ** -0.5)
    inp['hgrn_lb'] = nrm((2, N_HGRN_LAYERS + 1, A_WIDTH), 1.0)
    inp['hgrn_gnorm_0'] = gain((A_DV,))
    inp['gmlp_vnorm_0'] = gain((B_WIDTH,))
    inp['gmlp_ws_0'] = nrm((B_GROUPS, B_CHUNK, B_CHUNK), B_CHUNK ** -0.5)
    inp['gmlp_bs_0'] = nrm((B_GROUPS, B_CHUNK), 0.1)
    inp['w_out_0'] = nrm((A_WIDTH + B_WIDTH, d), (A_WIDTH + B_WIDTH) ** -0.5)
    inp['norm2_0'] = gain((d,))
    inp['ffn_up_0'] = nrm((d, 2 * D_FF), d ** -0.5)
    inp['ffn_conv_w_0'] = nrm((CONV_W, 2 * D_FF), CONV_W ** -0.5)
    inp['ffn_conv_b_0'] = nrm((2 * D_FF,), 0.01)
    inp['ffn_down_0'] = nrm((D_FF, d), D_FF ** -0.5)
    inp['mod_w_1'] = nrm((d, 6 * d), 0.5 * d ** -0.5)
    inp['mod_b_1'] = nrm((6 * d,), 0.01)
    inp['norm1_1'] = gain((d,))
    inp['w_out_1'] = nrm((d, d), d ** -0.5)
    inp['norm2_1'] = gain((d,))
    inp['ffn_up_1'] = nrm((d, 2 * D_FF), d ** -0.5)
    inp['ffn_conv_w_1'] = nrm((CONV_W, 2 * D_FF), CONV_W ** -0.5)
    inp['ffn_conv_b_1'] = nrm((2 * D_FF,), 0.01)
    inp['ffn_down_1'] = nrm((D_FF, d), D_FF ** -0.5)
    inp['final_norm'] = gain((d,))
    return inp


def reference(x_prompt, x_sample, state_l0_hgrn, c, c_ctx,
              mod_w_0, mod_b_0, norm1_0, w_in_0, hgrn_lb, hgrn_gnorm_0, gmlp_vnorm_0, gmlp_ws_0, gmlp_bs_0,
              w_out_0, norm2_0, ffn_up_0, ffn_conv_w_0, ffn_conv_b_0, ffn_down_0,
              mod_w_1, mod_b_1, norm1_1, w_out_1, norm2_1, ffn_up_1, ffn_conv_w_1, ffn_conv_b_1, ffn_down_1,
              final_norm):
    layers = [
        {'mod_w': mod_w_0, 'mod_b': mod_b_0, 'norm1': norm1_0, 'norm2': norm2_0, 'up': ffn_up_0,
         'cw': ffn_conv_w_0, 'cb': ffn_conv_b_0, 'down': ffn_down_0, 'w_in': w_in_0, 'gnorm': hgrn_gnorm_0,
         'vnorm': gmlp_vnorm_0, 'ws': gmlp_ws_0, 'bs': gmlp_bs_0, 'w_out': w_out_0},
        {'mod_w': mod_w_1, 'mod_b': mod_b_1, 'norm1': norm1_1, 'norm2': norm2_1, 'up': ffn_up_1,
         'cw': ffn_conv_w_1, 'cb': ffn_conv_b_1, 'down': ffn_down_1, 'w_out': w_out_1},
    ]
    lb_all = jnp.cumsum(jax.nn.softmax(hgrn_lb.astype(jnp.float32), axis=1), axis=1)

    def run_trunk(x, cond, s0_list, n_rows):
        states = []
        for l in range(DEPTH):
            p = layers[l]
            mod = (jax.nn.silu(cond) @ p['mod_w'] + p['mod_b'])[:, None, :]
            sh1, sc1, g1, sh2, sc2, g2 = jnp.split(mod, 6, axis=-1)
            h = _rmsnorm(x, p['norm1']) * (1.0 + sc1) + sh1
            if l % 2 == 0:
                j = l // 2
                mix, s = _mixers_ab(h, p['w_in'], lb_all[0, j], lb_all[1, j], p['gnorm'], p['vnorm'],
                                    p['ws'], p['bs'], p['w_out'], s0_list[j])
                states.append(s)
            else:
                mix = _fourier_mix(h) @ p['w_out']
            x = x + g1 * mix
            h = _rmsnorm(x, p['norm2']) * (1.0 + sc2) + sh2
            x = x + g2 * _conv_ffn(h, p['up'], p['cw'], p['cb'], p['down'], n_rows)
        return _rmsnorm(x, final_norm), states

    zero_state = jnp.zeros((x_prompt.shape[0], 2, A_HEADS, A_DK, A_DV), jnp.float32)
    y_prompt, ctx_states = run_trunk(x_prompt, c_ctx[None, :], [zero_state] * N_HGRN_LAYERS, 1)
    state_l0_hgrn_new = ctx_states[0].astype(x_prompt.dtype)

    n_rows = x_sample.shape[1] // GRID_W
    y_sample, _ = run_trunk(x_sample, c, [state_l0_hgrn], n_rows)
    return (y_prompt, y_sample, state_l0_hgrn_new)
```

```python
import functools

import numpy as np
import jax
import jax.numpy as jnp
from jax import lax
from jax.experimental import pallas as pl
from jax.experimental.pallas import tpu as pltpu

D_MODEL = 2048
A_WIDTH = 1024
A_HEADS = 8
A_DK = 128
A_DV = 128
B_WIDTH = 1024
B_GROUPS = 4
B_CHUNK = 128
B_CG = B_WIDTH // B_GROUPS
C_GROUPS = 4
C_CG = D_MODEL // C_GROUPS
SCAN_CHUNK = 64
D_FF = 5632
GRID_W = 64
IN_WIDTH = 5 * A_WIDTH + 2 * B_WIDTH
EPS = 1e-6

F32 = jnp.float32
BF16 = jnp.bfloat16

VMEM_LIMIT = 56 * 1024 * 1024
MOD_ROWS = 16


def _params(*sem):
    return pltpu.CompilerParams(dimension_semantics=sem, vmem_limit_bytes=VMEM_LIMIT)


def _sigmoid(x):
    return 1.0 / (1.0 + jnp.exp(-x))


def _silu(x):
    return x * _sigmoid(x)


def _gelu_tanh(x):
    c = np.float32(np.sqrt(2.0 / np.pi))
    return 0.5 * x * (1.0 + jnp.tanh(c * (x + 0.044715 * (x * x * x))))


def _rms(x, g):
    return x * lax.rsqrt(jnp.mean(x * x, axis=-1, keepdims=True) + EPS) * g


def _dot(a, b):
    return jnp.dot(a, b, preferred_element_type=F32)


def _dot_nt(a, b):
    return lax.dot_general(a, b, (((1,), (1,)), ((), ())), preferred_element_type=F32)


def _mod_kernel(c_ref, w_ref, b_ref, o_ref):
    s = _silu(c_ref[...]).astype(BF16)
    o_ref[...] = _dot(s, w_ref[...].astype(BF16)) + b_ref[...]


def _modulation(cond, w, b):
    tn = 1024
    n = w.shape[1]
    return pl.pallas_call(
        _mod_kernel,
        grid=(n // tn,),
        in_specs=[
            pl.BlockSpec((MOD_ROWS, D_MODEL), lambda j: (0, 0)),
            pl.BlockSpec((D_MODEL, tn), lambda j: (0, j)),
            pl.BlockSpec((1, tn), lambda j: (0, j)),
        ],
        out_specs=pl.BlockSpec((MOD_ROWS, tn), lambda j: (0, j)),
        out_shape=jax.ShapeDtypeStruct((MOD_ROWS, n), F32),
        compiler_params=_params("arbitrary"),
        name="modulation",
    )(cond, w, b.reshape(1, n))


def _mod_spec(which, tm, rows_per_mod, mod_off):
    return pl.BlockSpec(
        (None, None, 1, D_MODEL),
        lambda i, *_: (mod_off + (i * tm) // rows_per_mod, which, 0, 0))


def _norm_proj_kernel(x_ref, g_ref, sc_ref, sh_ref, w_ref, o_ref, h_scr):
    @pl.when(pl.program_id(1) == 0)
    def _():
        h = _rms(x_ref[...], g_ref[...]) * (1.0 + sc_ref[...]) + sh_ref[...]
        h_scr[...] = h.astype(BF16)

    o_ref[...] = _dot(h_scr[...], w_ref[...]).astype(o_ref.dtype)


def _norm_proj(x, g, mod4, sc_idx, sh_idx, w, rows_per_mod, mod_off, tm, tn, out_dtype):
    m = x.shape[0]
    n = w.shape[1]
    return pl.pallas_call(
        _norm_proj_kernel,
        grid=(m // tm, n // tn),
        in_specs=[
            pl.BlockSpec((tm, D_MODEL), lambda i, j: (i, 0)),
            pl.BlockSpec((1, D_MODEL), lambda i, j: (0, 0)),
            _mod_spec(sc_idx, tm, rows_per_mod, mod_off),
            _mod_spec(sh_idx, tm, rows_per_mod, mod_off),
            pl.BlockSpec((D_MODEL, tn), lambda i, j: (0, j)),
        ],
        out_specs=pl.BlockSpec((tm, tn), lambda i, j: (i, j)),
        out_shape=jax.ShapeDtypeStruct((m, n), out_dtype),
        scratch_shapes=[pltpu.VMEM((tm, D_MODEL), BF16)],
        compiler_params=_params("parallel", "arbitrary"),
        name="norm_proj",
    )(x, g.reshape(1, D_MODEL), mod4, mod4, w)


def _gla_kernel(qa_ref, fzf_ref, fzb_ref, ia_ref, ga_ref, lbp_ref, gn_ref, s0_ref,
                out_ref, *rest, n, emit_state):
    if emit_state:
        snew_ref, q_scr, of_scr, st_scr = rest
    else:
        q_scr, of_scr, st_scr = rest
    c_len = SCAN_CHUNK
    nc = n // c_len
    mid = c_len // 2

    q_scr[...] = _silu(qa_ref[...]) * (A_DK ** -0.5)

    row = lax.broadcasted_iota(jnp.int32, (c_len, A_DK), 0)
    ri = lax.broadcasted_iota(jnp.int32, (c_len, c_len), 0)
    ci = lax.broadcasted_iota(jnp.int32, (c_len, c_len), 1)

    def lower_bound(d):
        a = [lbp_ref[d, l] for l in range(lbp_ref.shape[1])]
        mx = functools.reduce(jnp.maximum, a)
        e = [jnp.exp(t - mx) for t in a]
        return e[0] / functools.reduce(lambda u, w: u + w, e)

    def chunk(r0, fz_ref, lb, reverse):
        qc = q_scr[pl.ds(r0, c_len), :]
        vc = ia_ref[pl.ds(r0, c_len), :]
        f = lb + (1.0 - lb) * _sigmoid(fz_ref[pl.ds(r0, c_len), :])
        kc = 1.0 - f
        b = jnp.log(f)
        for s in (1, 2, 4, 8, 16, 32):
            if reverse:
                b = b + jnp.where(row < c_len - s, pltpu.roll(b, c_len - s, axis=0), 0.0)
            else:
                b = b + jnp.where(row >= s, pltpu.roll(b, s, axis=0), 0.0)
        if reverse:
            ref = b[c_len - mid:c_len - mid + 1, :]
            b_last = b[0:1, :]
            keep = ri <= ci
        else:
            ref = b[mid - 1:mid, :]
            b_last = b[c_len - 1:c_len, :]
            keep = ri >= ci
        scores = _dot_nt((qc * jnp.exp(b - ref)).astype(BF16), (kc * jnp.exp(ref - b)).astype(BF16))
        scores = jnp.where(keep, scores, 0.0)
        vb = vc.astype(BF16)
        o = _dot(scores.astype(BF16), vb)
        st = st_scr[...]
        o = o + _dot_nt((qc * jnp.exp(b)).astype(BF16), st.astype(BF16))
        k_out = (kc * jnp.exp(b_last - b)).astype(BF16)
        st_scr[...] = jnp.exp(b_last) * st + _dot(vc.T.astype(BF16), k_out)
        return o

    lb_f = lower_bound(0)
    lb_b = lower_bound(1)

    st_scr[...] = s0_ref[0].T

    def fwd(c, carry):
        r0 = pl.multiple_of(c * c_len, c_len)
        of_scr[pl.ds(r0, c_len), :] = chunk(r0, fzf_ref, lb_f, False)
        return carry

    lax.fori_loop(0, nc, fwd, 0)
    if emit_state:
        snew_ref[0] = st_scr[...].T

    st_scr[...] = s0_ref[1].T

    def bwd(t, carry):
        r0 = pl.multiple_of((nc - 1 - t) * c_len, c_len)
        o = chunk(r0, fzb_ref, lb_b, True) + of_scr[pl.ds(r0, c_len), :]
        o = _rms(o, gn_ref[...]) * _silu(ga_ref[pl.ds(r0, c_len), :])
        out_ref[pl.ds(r0, c_len), :] = o.astype(out_ref.dtype)
        return carry

    lax.fori_loop(0, nc, bwd, 0)
    if emit_state:
        snew_ref[1] = st_scr[...].T


def _gla(proj3, hgrn_lb, gnorm, s0, emit_state):
    bsz, n, _ = proj3.shape
    h = A_HEADS

    def col(k):
        return pl.BlockSpec((None, n, A_DK), lambda b, hh: (b, 0, k * h + hh))

    n_lb = hgrn_lb.shape[1]
    lb_spec = pl.BlockSpec((2, n_lb, None, 1, A_DK), lambda b, hh: (0, 0, hh, 0, 0))
    st_spec = pl.BlockSpec((None, 2, None, A_DK, A_DV), lambda b, hh: (b, 0, hh, 0, 0))
    out_shape = [jax.ShapeDtypeStruct((bsz, n, A_WIDTH), BF16)]
    out_specs = [pl.BlockSpec((None, n, A_DV), lambda b, hh: (b, 0, hh))]
    if emit_state:
        out_shape.append(jax.ShapeDtypeStruct((bsz, 2, h, A_DK, A_DV), F32))
        out_specs.append(st_spec)
    res = pl.pallas_call(
        functools.partial(_gla_kernel, n=n, emit_state=emit_state),
        grid=(bsz, h),
        in_specs=[col(0), col(1), col(2), col(3), col(4), lb_spec,
                  pl.BlockSpec((1, A_DV), lambda b, hh: (0, 0)), st_spec],
        out_specs=out_specs,
        out_shape=out_shape,
        scratch_shapes=[pltpu.VMEM((n, A_DK), F32), pltpu.VMEM((n, A_DV), F32),
                        pltpu.VMEM((A_DV, A_DK), F32)],
        compiler_params=_params("parallel", "parallel"),
        name="hgrn2",
    )(proj3, proj3, proj3, proj3, proj3, hgrn_lb.reshape(2, n_lb, h, 1, A_DK),
      gnorm.reshape(1, A_DV), s0)
    return res


def _gmlp_kernel(ub_ref, vb_ref, vn_ref, ws_ref, bst_ref, out_ref, *, rows):
    for g in range(B_GROUPS):
        cs = slice(g * B_CG, (g + 1) * B_CG)
        vv = _rms(_gelu_tanh(vb_ref[:, cs]), vn_ref[:, cs]).astype(BF16)
        w = ws_ref[g].astype(BF16)
        bias = bst_ref[:, g:g + 1]
        for c in range(rows // B_CHUNK):
            rs = slice(c * B_CHUNK, (c + 1) * B_CHUNK)
            mixed = _dot(w, vv[rs, :]) + bias
            out_ref[rs, cs] = (_gelu_tanh(ub_ref[rs, cs]) * mixed).astype(out_ref.dtype)


def _gmlp(proj, vnorm, ws, bs, rows):
    m = proj.shape[0]
    ub_blk = 5 * A_WIDTH // B_WIDTH
    return pl.pallas_call(
        functools.partial(_gmlp_kernel, rows=rows),
        grid=(m // rows,),
        in_specs=[
            pl.BlockSpec((rows, B_WIDTH), lambda i: (i, ub_blk)),
            pl.BlockSpec((rows, B_WIDTH), lambda i: (i, ub_blk + 1)),
            pl.BlockSpec((1, B_WIDTH), lambda i: (0, 0)),
            pl.BlockSpec((B_GROUPS, B_CHUNK, B_CHUNK), lambda i: (0, 0, 0)),
            pl.BlockSpec((B_CHUNK, B_GROUPS), lambda i: (0, 0)),
        ],
        out_specs=pl.BlockSpec((rows, B_WIDTH), lambda i: (i, 0)),
        out_shape=jax.ShapeDtypeStruct((m, B_WIDTH), BF16),
        compiler_params=_params("parallel"),
        name="gmlp",
    )(proj, proj, vnorm.reshape(1, B_WIDTH), ws, bs.T)


def _resid_proj_kernel(*refs, n_parts):
    x_ref, gate_ref = refs[0], refs[1]
    a_refs = refs[2:2 + n_parts]
    w_refs = refs[2 + n_parts:2 + 2 * n_parts]
    o_ref = refs[2 + 2 * n_parts]
    acc = _dot(a_refs[0][...], w_refs[0][...])
    for a_ref, w_ref in zip(a_refs[1:], w_refs[1:]):
        acc = acc + _dot(a_ref[...], w_ref[...])
    o_ref[...] = x_ref[...] + gate_ref[...] * acc


def _resid_proj(x, mod4, gate_idx, parts, w, rows_per_mod, mod_off, tm):
    m = x.shape[0]
    n_parts = len(parts)
    kp = parts[0].shape[1]
    in_specs = [pl.BlockSpec((tm, D_MODEL), lambda i: (i, 0)),
                _mod_spec(gate_idx, tm, rows_per_mod, mod_off)]
    in_specs += [pl.BlockSpec((tm, kp), lambda i: (i, 0)) for _ in parts]
    in_specs += [pl.BlockSpec((kp, D_MODEL), lambda i, p=p: (p, 0)) for p in range(n_parts)]
    return pl.pallas_call(
        functools.partial(_resid_proj_kernel, n_parts=n_parts),
        grid=(m // tm,),
        in_specs=in_specs,
        out_specs=pl.BlockSpec((tm, D_MODEL), lambda i: (i, 0)),
        out_shape=jax.ShapeDtypeStruct((m, D_MODEL), F32),
        compiler_params=_params("parallel"),
        name="resid_proj",
    )(x, mod4, *parts, *([w] * n_parts))


def _dft_tables(n):
    idx = np.arange(n, dtype=np.int64)
    ang = 2.0 * np.pi * ((idx[:, None] * idx[None, :]) % n).astype(np.float64) / n
    s = 1.0 / np.sqrt(n)
    return (np.cos(ang) * s).astype(np.float32), (np.sin(ang) * s).astype(np.float32)


def _norm_chan_dft_kernel(x_ref, g_ref, sc_ref, sh_ref, cc_ref, sn_ref, p_ref, q_ref, h_scr):
    @pl.when(pl.program_id(1) == 0)
    def _():
        h = _rms(x_ref[...], g_ref[...]) * (1.0 + sc_ref[...]) + sh_ref[...]
        for g in range(C_GROUPS):
            h_scr[g] = h[:, g * C_CG:(g + 1) * C_CG].astype(BF16)

    hg = h_scr[pl.program_id(1)]
    p_ref[...] = _dot(hg, cc_ref[...]).astype(p_ref.dtype)
    q_ref[...] = _dot(hg, sn_ref[...]).astype(q_ref.dtype)


def _norm_chan_dft(x, g, mod4, sc_idx, sh_idx, rows_per_mod, mod_off, tm):
    m = x.shape[0]
    cc, sn = _dft_tables(C_CG)
    cc = jnp.asarray(cc).astype(BF16)
    sn = jnp.asarray(sn).astype(BF16)
    tab = pl.BlockSpec((C_CG, C_CG), lambda i, j: (0, 0))
    out = pl.BlockSpec((tm, C_CG), lambda i, j: (i, j))
    return pl.pallas_call(
        _norm_chan_dft_kernel,
        grid=(m // tm, C_GROUPS),
        in_specs=[
            pl.BlockSpec((tm, D_MODEL), lambda i, j: (i, 0)),
            pl.BlockSpec((1, D_MODEL), lambda i, j: (0, 0)),
            _mod_spec(sc_idx, tm, rows_per_mod, mod_off),
            _mod_spec(sh_idx, tm, rows_per_mod, mod_off),
            tab, tab,
        ],
        out_specs=[out, out],
        out_shape=[jax.ShapeDtypeStruct((m, D_MODEL), BF16)] * 2,
        scratch_shapes=[pltpu.VMEM((C_GROUPS, tm, C_CG), BF16)],
        compiler_params=_params("parallel", "arbitrary"),
        name="norm_chan_dft",
    )(x, g.reshape(1, D_MODEL), mod4, mod4, cc, sn)


def _pos_dft_kernel(cn_ref, sn_ref, p_ref, q_ref, o_ref):
    o_ref[...] = (_dot(cn_ref[...], p_ref[...]) - _dot(sn_ref[...], q_ref[...])).astype(o_ref.dtype)


def _pos_dft(p3, q3, tn):
    bsz, n, _ = p3.shape
    cn, sn = _dft_tables(n)
    cn = jnp.asarray(cn).astype(BF16)
    sn = jnp.asarray(sn).astype(BF16)
    tab = pl.BlockSpec((n, n), lambda b, j: (0, 0))
    blk = pl.BlockSpec((None, n, tn), lambda b, j: (b, 0, j))
    return pl.pallas_call(
        _pos_dft_kernel,
        grid=(bsz, D_MODEL // tn),
        in_specs=[tab, tab, blk, blk],
        out_specs=blk,
        out_shape=jax.ShapeDtypeStruct((bsz, n, D_MODEL), BF16),
        compiler_params=_params("parallel", "parallel"),
        name="pos_dft",
    )(cn, sn, p3, q3)


def _ffn_kernel(x_ref, g_ref, sc_ref, sh_ref, gate_ref, wg_ref, wv_ref, cwg_ref, cwv_ref,
                cbg_ref, cbv_ref, wd_ref, fin_ref, o_ref, h_scr, acc_scr, *, row_len, final_norm):
    j = pl.program_id(1)
    tm = x_ref.shape[0]

    @pl.when(j == 0)
    def _():
        h = _rms(x_ref[...], g_ref[...]) * (1.0 + sc_ref[...]) + sh_ref[...]
        h_scr[...] = h.astype(BF16)
        acc_scr[...] = jnp.zeros_like(acc_scr)

    h = h_scr[...]
    pos = lax.broadcasted_iota(jnp.int32, (tm, wg_ref.shape[1]), 0) % row_len
    has_prev = pos != 0
    has_next = pos != row_len - 1

    def conv(w_ref, cw_ref, cb_ref):
        up = _dot(h, w_ref[...])
        prev = jnp.where(has_prev, pltpu.roll(up, 1, axis=0), 0.0)
        nxt = jnp.where(has_next, pltpu.roll(up, tm - 1, axis=0), 0.0)
        return prev * cw_ref[0:1, :] + up * cw_ref[1:2, :] + nxt * cw_ref[2:3, :] + cb_ref[...]

    act = _silu(conv(wg_ref, cwg_ref, cbg_ref)) * conv(wv_ref, cwv_ref, cbv_ref)
    acc_scr[...] += _dot(act.astype(BF16), wd_ref[...])

    @pl.when(j == pl.num_programs(1) - 1)
    def _():
        y = x_ref[...] + gate_ref[...] * acc_scr[...]
        if final_norm:
            y = _rms(y, fin_ref[...])
        o_ref[...] = y


def _conv_ffn(x, g, mod4, w_up, cw, cb, w_down, fin, rows_per_mod, mod_off, row_len, tm, tf,
              final_norm):
    m = x.shape[0]
    nf = D_FF // tf
    cb2 = cb.reshape(1, 2 * D_FF)
    vec = pl.BlockSpec((1, D_MODEL), lambda i, j: (0, 0))
    return pl.pallas_call(
        functools.partial(_ffn_kernel, row_len=row_len, final_norm=final_norm),
        grid=(m // tm, nf),
        in_specs=[
            pl.BlockSpec((tm, D_MODEL), lambda i, j: (i, 0)),
            vec,
            _mod_spec(4, tm, rows_per_mod, mod_off),
            _mod_spec(3, tm, rows_per_mod, mod_off),
            _mod_spec(5, tm, rows_per_mod, mod_off),
            pl.BlockSpec((D_MODEL, tf), lambda i, j: (0, j)),
            pl.BlockSpec((D_MODEL, tf), lambda i, j: (0, nf + j)),
            pl.BlockSpec((3, tf), lambda i, j: (0, j)),
            pl.BlockSpec((3, tf), lambda i, j: (0, nf + j)),
            pl.BlockSpec((1, tf), lambda i, j: (0, j)),
            pl.BlockSpec((1, tf), lambda i, j: (0, nf + j)),
            pl.BlockSpec((tf, D_MODEL), lambda i, j: (j, 0)),
            vec,
        ],
        out_specs=pl.BlockSpec((tm, D_MODEL), lambda i, j: (i, 0)),
        out_shape=jax.ShapeDtypeStruct((m, D_MODEL), F32),
        scratch_shapes=[pltpu.VMEM((tm, D_MODEL), BF16), pltpu.VMEM((tm, D_MODEL), F32)],
        compiler_params=_params("parallel", "arbitrary"),
        name="conv_ffn",
    )(x, g.reshape(1, D_MODEL), mod4, mod4, mod4, w_up, w_up, cw, cw, cb2, cb2, w_down,
      fin.reshape(1, D_MODEL))


def kernel(x_prompt, x_sample, state_l0_hgrn, c, c_ctx, mod_w_0, mod_b_0, norm1_0, w_in_0, hgrn_lb, hgrn_gnorm_0, gmlp_vnorm_0, gmlp_ws_0, gmlp_bs_0, w_out_0, norm2_0, ffn_up_0, ffn_conv_w_0, ffn_conv_b_0, ffn_down_0, mod_w_1, mod_b_1, norm1_1, w_out_1, norm2_1, ffn_up_1, ffn_conv_w_1, ffn_conv_b_1, ffn_down_1, final_norm):
    n_dec = c.shape[0]
    cond = jnp.concatenate(
        [c, c_ctx[None, :], jnp.zeros((MOD_ROWS - n_dec - 1, D_MODEL), F32)], axis=0)
    mod4 = [_modulation(cond, w, b).reshape(MOD_ROWS, 6, 1, D_MODEL)
            for w, b in ((mod_w_0, mod_b_0), (mod_w_1, mod_b_1))]

    w_in = w_in_0.astype(BF16)
    w_out0 = w_out_0.astype(BF16)
    w_out1 = w_out_1.astype(BF16)
    ffn = [(norm2_0, ffn_up_0.astype(BF16), ffn_conv_w_0, ffn_conv_b_0, ffn_down_0.astype(BF16)),
           (norm2_1, ffn_up_1.astype(BF16), ffn_conv_w_1, ffn_conv_b_1, ffn_down_1.astype(BF16))]

    def trunk(x3, s0, mod_off, per_batch_mod, row_len, emit_state):
        bsz, n, _ = x3.shape
        m = bsz * n
        rows_per_mod = n if per_batch_mod else m
        tm = 512
        x = x3.reshape(m, D_MODEL)

        proj = _norm_proj(x, norm1_0, mod4[0], 1, 0, w_in, rows_per_mod, mod_off, tm, 1024, F32)
        res = _gla(proj.reshape(bsz, n, IN_WIDTH), hgrn_lb, hgrn_gnorm_0, s0,
                   emit_state)
        out_b = _gmlp(proj, gmlp_vnorm_0, gmlp_ws_0, gmlp_bs_0, 512)
        x = _resid_proj(x, mod4[0], 2, [res[0].reshape(m, A_WIDTH), out_b], w_out0,
                        rows_per_mod, mod_off, tm)
        x = _conv_ffn(x, *ffn[0][:1], mod4[0], *ffn[0][1:], final_norm, rows_per_mod, mod_off,
                      row_len, tm, 512, False)

        p, q = _norm_chan_dft(x, norm1_1, mod4[1], 1, 0, rows_per_mod, mod_off, tm)
        four = _pos_dft(p.reshape(bsz, n, D_MODEL), q.reshape(bsz, n, D_MODEL), 512)
        x = _resid_proj(x, mod4[1], 2, [four.reshape(m, D_MODEL)], w_out1, rows_per_mod, mod_off, tm)
        x = _conv_ffn(x, *ffn[1][:1], mod4[1], *ffn[1][1:], final_norm, rows_per_mod, mod_off,
                      row_len, tm, 512, True)
        return x.reshape(bsz, n, D_MODEL), (res[1] if emit_state else None)

    zero_state = jnp.zeros((x_prompt.shape[0], 2, A_HEADS, A_DK, A_DV), F32)
    y_prompt, state_new = trunk(x_prompt, zero_state, n_dec, False, x_prompt.shape[1], True)
    y_sample, _ = trunk(x_sample, state_l0_hgrn, 0, True, GRID_W, False)
    return (y_prompt, y_sample, state_new.astype(x_prompt.dtype))
```

```python
import functools

import numpy as np
import jax
import jax.numpy as jnp
from jax import lax
from jax.experimental import pallas as pl
from jax.experimental.pallas import tpu as pltpu

D_MODEL = 2048
A_WIDTH = 1024
A_HEADS = 8
A_DK = 128
A_DV = 128
B_WIDTH = 1024
B_GROUPS = 4
B_CHUNK = 128
B_CG = B_WIDTH // B_GROUPS
C_GROUPS = 4
C_CG = D_MODEL // C_GROUPS
SCAN_CHUNK = 64
D_FF = 5632
GRID_W = 64
IN_WIDTH = 5 * A_WIDTH + 2 * B_WIDTH
EPS = 1e-6

F32 = jnp.float32
BF16 = jnp.bfloat16

VMEM_LIMIT = 56 * 1024 * 1024
MOD_ROWS = 16
GLA_ROW_BLOCK = 256
FFN_CONV_ROWS = 64
FFN_MXU_PIECES = 4
HALO = 8


def _params(*sem):
    return pltpu.CompilerParams(dimension_semantics=sem, vmem_limit_bytes=VMEM_LIMIT)


def _sigmoid(x):
    return 1.0 / (1.0 + jnp.exp(-x))


def _silu(x):
    return x * _sigmoid(x)


def _gelu_tanh(x):
    c = np.float32(np.sqrt(2.0 / np.pi))
    return 0.5 * x * (1.0 + jnp.tanh(c * (x + 0.044715 * (x * x * x))))


def _rms(x, g):
    return x * lax.rsqrt(jnp.mean(x * x, axis=-1, keepdims=True) + EPS) * g


def _dot(a, b):
    return jnp.dot(a, b, preferred_element_type=F32)


def _dot_nt(a, b):
    return lax.dot_general(a, b, (((1,), (1,)), ((), ())), preferred_element_type=F32)


def _mod_kernel(c_ref, w_ref, b_ref, o_ref):
    s = _silu(c_ref[...]).astype(BF16)
    o_ref[...] = _dot(s, w_ref[...].astype(BF16)) + b_ref[...]


def _modulation(cond, w, b):
    tn = 1024
    n = w.shape[1]
    return pl.pallas_call(
        _mod_kernel,
        grid=(n // tn,),
        in_specs=[
            pl.BlockSpec((MOD_ROWS, D_MODEL), lambda j: (0, 0)),
            pl.BlockSpec((D_MODEL, tn), lambda j: (0, j)),
            pl.BlockSpec((1, tn), lambda j: (0, j)),
        ],
        out_specs=pl.BlockSpec((MOD_ROWS, tn), lambda j: (0, j)),
        out_shape=jax.ShapeDtypeStruct((MOD_ROWS, n), F32),
        compiler_params=_params("arbitrary"),
        name="modulation",
    )(cond, w, b.reshape(1, n))


def _mod_spec(which, tm, rows_per_mod, mod_off):
    return pl.BlockSpec(
        (None, None, 1, D_MODEL),
        lambda i, *_: (mod_off + (i * tm) // rows_per_mod, which, 0, 0))


def _norm_proj_kernel(x_ref, g_ref, sc_ref, sh_ref, w_ref, o_ref, h_scr):
    @pl.when(pl.program_id(1) == 0)
    def _():
        h = _rms(x_ref[...], g_ref[...]) * (1.0 + sc_ref[...]) + sh_ref[...]
        h_scr[...] = h.astype(BF16)

    o_ref[...] = _dot(h_scr[...], w_ref[...]).astype(o_ref.dtype)


def _norm_proj(x, g, mod4, sc_idx, sh_idx, w, rows_per_mod, mod_off, tm, tn, out_dtype):
    m = x.shape[0]
    n = w.shape[1]
    return pl.pallas_call(
        _norm_proj_kernel,
        grid=(m // tm, n // tn),
        in_specs=[
            pl.BlockSpec((tm, D_MODEL), lambda i, j: (i, 0)),
            pl.BlockSpec((1, D_MODEL), lambda i, j: (0, 0)),
            _mod_spec(sc_idx, tm, rows_per_mod, mod_off),
            _mod_spec(sh_idx, tm, rows_per_mod, mod_off),
            pl.BlockSpec((D_MODEL, tn), lambda i, j: (0, j)),
        ],
        out_specs=pl.BlockSpec((tm, tn), lambda i, j: (i, j)),
        out_shape=jax.ShapeDtypeStruct((m, n), out_dtype),
        scratch_shapes=[pltpu.VMEM((tm, D_MODEL), BF16)],
        compiler_params=_params("parallel", "arbitrary"),
        name="norm_proj",
    )(x, g.reshape(1, D_MODEL), mod4, mod4, w)


def _gla_kernel(qa_ref, fzf_ref, fzb_ref, ia_ref, ga_ref, lbp_ref, gn_ref, s0_ref,
                out_ref, *rest, n, emit_state):
    if emit_state:
        snew_ref, ops_scr, dec_scr, o_scr, st_scr = rest
    else:
        ops_scr, dec_scr, o_scr, st_scr = rest
    c_len = SCAN_CHUNK
    nc = n // c_len
    mid = c_len // 2
    rb = GLA_ROW_BLOCK
    cpb = rb // c_len

    def lower_bound(d):
        a = [lbp_ref[d, l] for l in range(lbp_ref.shape[1])]
        mx = functools.reduce(jnp.maximum, a)
        e = [jnp.exp(t - mx) for t in a]
        return e[0] / functools.reduce(lambda u, w: u + w, e)

    lbs = (lower_bound(0), lower_bound(1))
    pos = lax.broadcasted_iota(jnp.int32, (rb, A_DK), 0) % c_len

    def operands(i, carry):
        r0 = pl.multiple_of(i * rb, rb)
        rows = pl.ds(r0, rb)
        q = _silu(qa_ref[rows, :]) * (A_DK ** -0.5)
        for d, fz_ref in enumerate((fzf_ref, fzb_ref)):
            lb = lbs[d]
            f = lb + (1.0 - lb) * _sigmoid(fz_ref[rows, :])
            k = 1.0 - f
            b = jnp.log(f)
            for s in (1, 2, 4, 8, 16, 32):
                if d == 0:
                    b = b + jnp.where(pos >= s, pltpu.roll(b, s, axis=0), 0.0)
                else:
                    b = b + jnp.where(pos < c_len - s, pltpu.roll(b, rb - s, axis=0), 0.0)
            for ci in range(cpb):
                sl = slice(ci * c_len, (ci + 1) * c_len)
                bc = b[sl]
                if d == 0:
                    ref, b_last = bc[mid - 1:mid], bc[c_len - 1:c_len]
                else:
                    ref, b_last = bc[c_len - mid:c_len - mid + 1], bc[0:1]
                qe = q[sl] * jnp.exp(bc - ref)
                ke = k[sl] * jnp.exp(ref - bc)
                crow = pl.ds(r0 + ci * c_len, c_len)
                ops_scr[d, 0, crow, :] = qe.astype(BF16)
                ops_scr[d, 1, crow, :] = ke.astype(BF16)
                ops_scr[d, 2, crow, :] = (qe * jnp.exp(ref)).astype(BF16)
                ops_scr[d, 3, crow, :] = (ke * jnp.exp(b_last - ref)).astype(BF16)
                dec_scr[d, i * cpb + ci] = jnp.broadcast_to(jnp.exp(b_last), (8, A_DK))
        return carry

    lax.fori_loop(0, n // rb, operands, 0)

    ri = lax.broadcasted_iota(jnp.int32, (c_len, c_len), 0)
    ci_ = lax.broadcasted_iota(jnp.int32, (c_len, c_len), 1)
    keep = (ri >= ci_, ri <= ci_)
    for d in range(2):
        st_scr[d] = s0_ref[d].T

    def scan_step(c, carry):
        for d in range(2):
            cidx = c if d == 0 else nc - 1 - c
            rows = pl.ds(pl.multiple_of(cidx * c_len, c_len), c_len)
            vc = ia_ref[rows, :]
            scores = jnp.where(keep[d], _dot_nt(ops_scr[d, 0, rows, :], ops_scr[d, 1, rows, :]), 0.0)
            st = st_scr[d]
            o_scr[d, rows, :] = (_dot(scores.astype(BF16), vc.astype(BF16))
                                 + _dot_nt(ops_scr[d, 2, rows, :], st.astype(BF16)))
            st_scr[d] = dec_scr[d, cidx][0:1, :] * st + _dot(vc.T.astype(BF16), ops_scr[d, 3, rows, :])
        return carry

    lax.fori_loop(0, nc, scan_step, 0, unroll=True if nc <= 4 else 4)
    if emit_state:
        for d in range(2):
            snew_ref[d] = st_scr[d].T

    def finish(i, carry):
        rows = pl.ds(pl.multiple_of(i * rb, rb), rb)
        o = _rms(o_scr[0, rows, :] + o_scr[1, rows, :], gn_ref[...]) * _silu(ga_ref[rows, :])
        out_ref[rows, :] = o.astype(out_ref.dtype)
        return carry

    lax.fori_loop(0, n // rb, finish, 0)


def _gla(proj3, hgrn_lb, gnorm, s0, emit_state):
    bsz, n, _ = proj3.shape
    h = A_HEADS

    def col(k):
        return pl.BlockSpec((None, n, A_DK), lambda b, hh: (b, 0, k * h + hh))

    n_lb = hgrn_lb.shape[1]
    lb_spec = pl.BlockSpec((2, n_lb, None, 1, A_DK), lambda b, hh: (0, 0, hh, 0, 0))
    st_spec = pl.BlockSpec((None, 2, None, A_DK, A_DV), lambda b, hh: (b, 0, hh, 0, 0))
    out_shape = [jax.ShapeDtypeStruct((bsz, n, A_WIDTH), BF16)]
    out_specs = [pl.BlockSpec((None, n, A_DV), lambda b, hh: (b, 0, hh))]
    if emit_state:
        out_shape.append(jax.ShapeDtypeStruct((bsz, 2, h, A_DK, A_DV), F32))
        out_specs.append(st_spec)
    res = pl.pallas_call(
        functools.partial(_gla_kernel, n=n, emit_state=emit_state),
        grid=(bsz, h),
        in_specs=[col(0), col(1), col(2), col(3), col(4), lb_spec,
                  pl.BlockSpec((1, A_DV), lambda b, hh: (0, 0)), st_spec],
        out_specs=out_specs,
        out_shape=out_shape,
        scratch_shapes=[pltpu.VMEM((2, 4, n, A_DK), BF16),
                        pltpu.VMEM((2, n // SCAN_CHUNK, 8, A_DK), F32),
                        pltpu.VMEM((2, n, A_DV), F32),
                        pltpu.VMEM((2, A_DV, A_DK), F32)],
        compiler_params=_params("parallel", "parallel"),
        name="hgrn2",
    )(proj3, proj3, proj3, proj3, proj3, hgrn_lb.reshape(2, n_lb, h, 1, A_DK),
      gnorm.reshape(1, A_DV), s0)
    return res


def _gmlp_kernel(ub_ref, vb_ref, vn_ref, ws_ref, bst_ref, out_ref, *, rows):
    for g in range(B_GROUPS):
        cs = slice(g * B_CG, (g + 1) * B_CG)
        vv = _rms(_gelu_tanh(vb_ref[:, cs]), vn_ref[:, cs]).astype(BF16)
        w = ws_ref[g].astype(BF16)
        bias = bst_ref[:, g:g + 1]
        for c in range(rows // B_CHUNK):
            rs = slice(c * B_CHUNK, (c + 1) * B_CHUNK)
            mixed = _dot(w, vv[rs, :]) + bias
            out_ref[rs, cs] = (_gelu_tanh(ub_ref[rs, cs]) * mixed).astype(out_ref.dtype)


def _gmlp(proj, vnorm, ws, bs, rows):
    m = proj.shape[0]
    ub_blk = 5 * A_WIDTH // B_WIDTH
    return pl.pallas_call(
        functools.partial(_gmlp_kernel, rows=rows),
        grid=(m // rows,),
        in_specs=[
            pl.BlockSpec((rows, B_WIDTH), lambda i: (i, ub_blk)),
            pl.BlockSpec((rows, B_WIDTH), lambda i: (i, ub_blk + 1)),
            pl.BlockSpec((1, B_WIDTH), lambda i: (0, 0)),
            pl.BlockSpec((B_GROUPS, B_CHUNK, B_CHUNK), lambda i: (0, 0, 0)),
            pl.BlockSpec((B_CHUNK, B_GROUPS), lambda i: (0, 0)),
        ],
        out_specs=pl.BlockSpec((rows, B_WIDTH), lambda i: (i, 0)),
        out_shape=jax.ShapeDtypeStruct((m, B_WIDTH), BF16),
        compiler_params=_params("parallel"),
        name="gmlp",
    )(proj, proj, vnorm.reshape(1, B_WIDTH), ws, bs.T)


def _resid_proj_kernel(*refs, n_parts):
    x_ref, gate_ref = refs[0], refs[1]
    a_refs = refs[2:2 + n_parts]
    w_refs = refs[2 + n_parts:2 + 2 * n_parts]
    o_ref = refs[2 + 2 * n_parts]
    acc = _dot(a_refs[0][...], w_refs[0][...])
    for a_ref, w_ref in zip(a_refs[1:], w_refs[1:]):
        acc = acc + _dot(a_ref[...], w_ref[...])
    o_ref[...] = x_ref[...] + gate_ref[...] * acc


def _resid_proj(x, mod4, gate_idx, parts, w, rows_per_mod, mod_off, tm):
    m = x.shape[0]
    n_parts = len(parts)
    kp = parts[0].shape[1]
    in_specs = [pl.BlockSpec((tm, D_MODEL), lambda i: (i, 0)),
                _mod_spec(gate_idx, tm, rows_per_mod, mod_off)]
    in_specs += [pl.BlockSpec((tm, kp), lambda i: (i, 0)) for _ in parts]
    in_specs += [pl.BlockSpec((kp, D_MODEL), lambda i, p=p: (p, 0)) for p in range(n_parts)]
    return pl.pallas_call(
        functools.partial(_resid_proj_kernel, n_parts=n_parts),
        grid=(m // tm,),
        in_specs=in_specs,
        out_specs=pl.BlockSpec((tm, D_MODEL), lambda i: (i, 0)),
        out_shape=jax.ShapeDtypeStruct((m, D_MODEL), F32),
        compiler_params=_params("parallel"),
        name="resid_proj",
    )(x, mod4, *parts, *([w] * n_parts))


def _dft_tables(n):
    idx = np.arange(n, dtype=np.int64)
    ang = 2.0 * np.pi * ((idx[:, None] * idx[None, :]) % n).astype(np.float64) / n
    s = 1.0 / np.sqrt(n)
    return (np.cos(ang) * s).astype(np.float32), (np.sin(ang) * s).astype(np.float32)


def _norm_chan_dft_kernel(x_ref, g_ref, sc_ref, sh_ref, cc_ref, sn_ref, p_ref, q_ref, h_scr):
    @pl.when(pl.program_id(1) == 0)
    def _():
        h = _rms(x_ref[...], g_ref[...]) * (1.0 + sc_ref[...]) + sh_ref[...]
        for g in range(C_GROUPS):
            h_scr[g] = h[:, g * C_CG:(g + 1) * C_CG].astype(BF16)

    hg = h_scr[pl.program_id(1)]
    p_ref[...] = _dot(hg, cc_ref[...]).astype(p_ref.dtype)
    q_ref[...] = _dot(hg, sn_ref[...]).astype(q_ref.dtype)


def _norm_chan_dft(x, g, mod4, sc_idx, sh_idx, rows_per_mod, mod_off, tm):
    m = x.shape[0]
    cc, sn = _dft_tables(C_CG)
    cc = jnp.asarray(cc).astype(BF16)
    sn = jnp.asarray(sn).astype(BF16)
    tab = pl.BlockSpec((C_CG, C_CG), lambda i, j: (0, 0))
    out = pl.BlockSpec((tm, C_CG), lambda i, j: (i, j))
    return pl.pallas_call(
        _norm_chan_dft_kernel,
        grid=(m // tm, C_GROUPS),
        in_specs=[
            pl.BlockSpec((tm, D_MODEL), lambda i, j: (i, 0)),
            pl.BlockSpec((1, D_MODEL), lambda i, j: (0, 0)),
            _mod_spec(sc_idx, tm, rows_per_mod, mod_off),
            _mod_spec(sh_idx, tm, rows_per_mod, mod_off),
            tab, tab,
        ],
        out_specs=[out, out],
        out_shape=[jax.ShapeDtypeStruct((m, D_MODEL), BF16)] * 2,
        scratch_shapes=[pltpu.VMEM((C_GROUPS, tm, C_CG), BF16)],
        compiler_params=_params("parallel", "arbitrary"),
        name="norm_chan_dft",
    )(x, g.reshape(1, D_MODEL), mod4, mod4, cc, sn)


def _pos_dft_kernel(cn_ref, sn_ref, p_ref, q_ref, o_ref):
    o_ref[...] = (_dot(cn_ref[...], p_ref[...]) - _dot(sn_ref[...], q_ref[...])).astype(o_ref.dtype)


def _pos_dft(p3, q3, tn):
    bsz, n, _ = p3.shape
    cn, sn = _dft_tables(n)
    cn = jnp.asarray(cn).astype(BF16)
    sn = jnp.asarray(sn).astype(BF16)
    tab = pl.BlockSpec((n, n), lambda b, j: (0, 0))
    blk = pl.BlockSpec((None, n, tn), lambda b, j: (b, 0, j))
    return pl.pallas_call(
        _pos_dft_kernel,
        grid=(bsz, D_MODEL // tn),
        in_specs=[tab, tab, blk, blk],
        out_specs=blk,
        out_shape=jax.ShapeDtypeStruct((bsz, n, D_MODEL), BF16),
        compiler_params=_params("parallel", "parallel"),
        name="pos_dft",
    )(cn, sn, p3, q3)


def _ffn_kernel(xc_ref, g_ref, sc_ref, sh_ref, xp_ref, gate_ref, wg_ref, wv_ref, cwg_ref, cwv_ref,
                cbg_ref, cbv_ref, wd_ref, mask_ref, fin_ref, o_ref, h_scr, acc_scr, u0_scr, u1_scr,
                act0_scr, act1_scr, *, nf, n_tiles, final_norm):
    t = pl.program_id(0)
    tm, tf = act0_scr.shape
    lanes = mask_ref.shape[2]

    @pl.when(t == 0)
    def _():
        for ref in (u0_scr, u1_scr, act0_scr, act1_scr, acc_scr):
            ref[...] = jnp.zeros_like(ref)

    @pl.when((t < n_tiles) & (t % nf == 0))
    def _():
        h = _rms(xc_ref[...], g_ref[...]) * (1.0 + sc_ref[...]) + sh_ref[...]
        h_scr[...] = h.astype(BF16)

    @pl.when((t >= 2) & ((t - 2) % nf == 0))
    def _():
        acc_scr[...] = jnp.zeros_like(acc_scr)

    cb_rows = FFN_CONV_ROWS
    n_row_blocks = tm // cb_rows
    mask_rows = mask_ref.shape[1]

    def conv(u_old, k, r, cs, cw, cb):
        ext = u_old[k, r * cb_rows:(r + 1) * cb_rows + 2 * HALO, cs]
        n_ext = cb_rows + 2 * HALO
        ms = slice((r * cb_rows) % mask_rows, (r * cb_rows) % mask_rows + cb_rows)
        inner = slice(HALO, HALO + cb_rows)
        prev = pltpu.roll(ext, 1, axis=0)[inner] * mask_ref[0, ms, :]
        nxt = pltpu.roll(ext, n_ext - 1, axis=0)[inner] * mask_ref[1, ms, :]
        return prev * cw[0:1] + ext[inner] * cw[1:2] + nxt * cw[2:3] + cb

    def conv_block(u_old, act_new, idx):
        c, r = divmod(idx, n_row_blocks)
        cs = slice(c * lanes, (c + 1) * lanes)
        gate = conv(u_old, 0, r, cs, cwg_ref[:, cs], cbg_ref[:, cs])
        val = conv(u_old, 1, r, cs, cwv_ref[:, cs], cbv_ref[:, cs])
        act_new[r * cb_rows:(r + 1) * cb_rows, cs] = (_silu(gate) * val).astype(BF16)

    def stages(u_new, u_old, act_new, act_old):
        n_blocks = (tf // lanes) * n_row_blocks
        dn = D_MODEL // FFN_MXU_PIECES
        un = tf // (FFN_MXU_PIECES // 2)
        h = h_scr[...]
        act = act_old[...]
        n_pieces = 2 * FFN_MXU_PIECES
        work = [1] * FFN_MXU_PIECES + [2] * FFN_MXU_PIECES
        total = sum(work[:-1])
        done = 0
        for piece in range(n_pieces):
            if piece < FFN_MXU_PIECES:
                ns = slice(piece * dn, (piece + 1) * dn)
                acc_scr[:, ns] += _dot(act, wd_ref[:, ns])
            else:
                k, q = divmod(piece - FFN_MXU_PIECES, FFN_MXU_PIECES // 2)
                ns = slice(q * un, (q + 1) * un)
                w_ref = wg_ref if k == 0 else wv_ref
                u_new[k, HALO:HALO + tm, ns] = _dot(h, w_ref[:, ns])
            target = min(n_blocks, -(-n_blocks * sum(work[:piece + 1]) // total))
            for idx in range(done, target):
                conv_block(u_old, act_new, idx)
            done = target

    @pl.when(t % 2 == 0)
    def _():
        stages(u0_scr, u1_scr, act1_scr, act0_scr)

    @pl.when(t % 2 == 1)
    def _():
        stages(u1_scr, u0_scr, act0_scr, act1_scr)

    @pl.when((t >= 2) & ((t - 2) % nf == nf - 1))
    def _():
        y = xp_ref[...] + gate_ref[...] * acc_scr[...]
        if final_norm:
            y = _rms(y, fin_ref[...])
        o_ref[...] = y


def _conv_ffn(x, g, mod4, w_up, cw, cb, w_down, fin, rows_per_mod, mod_off, row_len, tm, tf,
              final_norm):
    m = x.shape[0]
    nf = D_FF // tf
    n_tiles = (m // tm) * nf
    cb2 = cb.reshape(1, 2 * D_FF)
    rb = max(row_len, FFN_CONV_ROWS)
    assert rb % row_len == 0 and rb % FFN_CONV_ROWS == 0 and tm % rb == 0
    pos = np.arange(rb) % row_len
    mask = np.stack([np.broadcast_to((pos != 0)[:, None], (rb, 128)),
                     np.broadcast_to((pos != row_len - 1)[:, None], (rb, 128))]).astype(np.float32)

    def up_tile(t):
        return jnp.minimum(t, n_tiles - 1)

    def conv_tile(t):
        return jnp.clip(t - 1, 0, n_tiles - 1)

    def down_tile(t):
        return jnp.clip(t - 2, 0, n_tiles - 1)

    def mod_spec(which, tile):
        return pl.BlockSpec(
            (None, None, 1, D_MODEL),
            lambda t: (mod_off + ((tile(t) // nf) * tm) // rows_per_mod, which, 0, 0))

    vec = pl.BlockSpec((1, D_MODEL), lambda t: (0, 0))
    return pl.pallas_call(
        functools.partial(_ffn_kernel, nf=nf, n_tiles=n_tiles, final_norm=final_norm),
        grid=(n_tiles + 2,),
        in_specs=[
            pl.BlockSpec((tm, D_MODEL), lambda t: (up_tile(t) // nf, 0)),
            vec,
            mod_spec(4, up_tile),
            mod_spec(3, up_tile),
            pl.BlockSpec((tm, D_MODEL), lambda t: (down_tile(t) // nf, 0)),
            mod_spec(5, down_tile),
            pl.BlockSpec((D_MODEL, tf), lambda t: (0, up_tile(t) % nf)),
            pl.BlockSpec((D_MODEL, tf), lambda t: (0, nf + up_tile(t) % nf)),
            pl.BlockSpec((3, tf), lambda t: (0, conv_tile(t) % nf)),
            pl.BlockSpec((3, tf), lambda t: (0, nf + conv_tile(t) % nf)),
            pl.BlockSpec((1, tf), lambda t: (0, conv_tile(t) % nf)),
            pl.BlockSpec((1, tf), lambda t: (0, nf + conv_tile(t) % nf)),
            pl.BlockSpec((tf, D_MODEL), lambda t: (down_tile(t) % nf, 0)),
            pl.BlockSpec((2, rb, 128), lambda t: (0, 0, 0)),
            vec,
        ],
        out_specs=pl.BlockSpec((tm, D_MODEL), lambda t: (down_tile(t) // nf, 0)),
        out_shape=jax.ShapeDtypeStruct((m, D_MODEL), F32),
        scratch_shapes=[pltpu.VMEM((tm, D_MODEL), BF16), pltpu.VMEM((tm, D_MODEL), F32),
                        pltpu.VMEM((2, tm + 2 * HALO, tf), F32),
                        pltpu.VMEM((2, tm + 2 * HALO, tf), F32),
                        pltpu.VMEM((tm, tf), BF16), pltpu.VMEM((tm, tf), BF16)],
        compiler_params=_params("arbitrary"),
        name="conv_ffn",
    )(x, g.reshape(1, D_MODEL), mod4, mod4, x, mod4, w_up, w_up, cw, cw, cb2, cb2, w_down,
      jnp.asarray(mask), fin.reshape(1, D_MODEL))


def kernel(x_prompt, x_sample, state_l0_hgrn, c, c_ctx, mod_w_0, mod_b_0, norm1_0, w_in_0, hgrn_lb, hgrn_gnorm_0, gmlp_vnorm_0, gmlp_ws_0, gmlp_bs_0, w_out_0, norm2_0, ffn_up_0, ffn_conv_w_0, ffn_conv_b_0, ffn_down_0, mod_w_1, mod_b_1, norm1_1, w_out_1, norm2_1, ffn_up_1, ffn_conv_w_1, ffn_conv_b_1, ffn_down_1, final_norm):
    n_dec = c.shape[0]
    cond = jnp.concatenate(
        [c, c_ctx[None, :], jnp.zeros((MOD_ROWS - n_dec - 1, D_MODEL), F32)], axis=0)
    mod4 = [_modulation(cond, w, b).reshape(MOD_ROWS, 6, 1, D_MODEL)
            for w, b in ((mod_w_0, mod_b_0), (mod_w_1, mod_b_1))]

    w_in = w_in_0.astype(BF16)
    w_out0 = w_out_0.astype(BF16)
    w_out1 = w_out_1.astype(BF16)
    ffn = [(norm2_0, ffn_up_0.astype(BF16), ffn_conv_w_0, ffn_conv_b_0, ffn_down_0.astype(BF16)),
           (norm2_1, ffn_up_1.astype(BF16), ffn_conv_w_1, ffn_conv_b_1, ffn_down_1.astype(BF16))]

    def trunk(x3, s0, mod_off, per_batch_mod, row_len, emit_state):
        bsz, n, _ = x3.shape
        m = bsz * n
        rows_per_mod = n if per_batch_mod else m
        tm = 512
        x = x3.reshape(m, D_MODEL)

        proj = _norm_proj(x, norm1_0, mod4[0], 1, 0, w_in, rows_per_mod, mod_off, tm, 1024, F32)
        res = _gla(proj.reshape(bsz, n, IN_WIDTH), hgrn_lb, hgrn_gnorm_0, s0,
                   emit_state)
        out_b = _gmlp(proj, gmlp_vnorm_0, gmlp_ws_0, gmlp_bs_0, 512)
        x = _resid_proj(x, mod4[0], 2, [res[0].reshape(m, A_WIDTH), out_b], w_out0,
                        rows_per_mod, mod_off, tm)
        x = _conv_ffn(x, *ffn[0][:1], mod4[0], *ffn[0][1:], final_norm, rows_per_mod, mod_off,
                      row_len, tm, 512, False)

        p, q = _norm_chan_dft(x, norm1_1, mod4[1], 1, 0, rows_per_mod, mod_off, tm)
        four = _pos_dft(p.reshape(bsz, n, D_MODEL), q.reshape(bsz, n, D_MODEL), 512)
        x = _resid_proj(x, mod4[1], 2, [four.reshape(m, D_MODEL)], w_out1, rows_per_mod, mod_off, tm)
        x = _conv_ffn(x, *ffn[1][:1], mod4[1], *ffn[1][1:], final_norm, rows_per_mod, mod_off,
                      row_len, tm, 512, True)
        return x.reshape(bsz, n, D_MODEL), (res[1] if emit_state else None)

    zero_state = jnp.zeros((x_prompt.shape[0], 2, A_HEADS, A_DK, A_DV), F32)
    y_prompt, state_new = trunk(x_prompt, zero_state, n_dec, False, x_prompt.shape[1], True)
    y_sample, _ = trunk(x_sample, state_l0_hgrn, 0, True, GRID_W, False)
    return (y_prompt, y_sample, state_new.astype(x_prompt.dtype))
```

```python
import functools

import numpy as np
import jax
import jax.numpy as jnp
from jax import lax
from jax.experimental import pallas as pl
from jax.experimental.pallas import tpu as pltpu

D_MODEL = 2048
A_WIDTH = 1024
A_HEADS = 8
A_DK = 128
A_DV = 128
B_WIDTH = 1024
B_GROUPS = 4
B_CHUNK = 128
B_CG = B_WIDTH // B_GROUPS
C_GROUPS = 4
C_CG = D_MODEL // C_GROUPS
SCAN_CHUNK = 64
D_FF = 5632
GRID_W = 64
IN_WIDTH = 5 * A_WIDTH + 2 * B_WIDTH
EPS = 1e-6

F32 = jnp.float32
BF16 = jnp.bfloat16

VMEM_LIMIT = 56 * 1024 * 1024
MOD_ROWS = 16
GLA_ROW_BLOCK = 256
FFN_CONV_ROWS = 64
FFN_MXU_PIECES = (4, 2)
FFN_TAIL_BLOCKS = 3
HALO = 8
ROW_TILE = 512
FFN_TF = 512
PROJ_ROW_TILE = 1024
PROJ_COL_TILE = 1024
POS_DFT_COL_TILE = 512


def _params(*sem):
    return pltpu.CompilerParams(dimension_semantics=sem, vmem_limit_bytes=VMEM_LIMIT)


def _sigmoid(x):
    return 1.0 / (1.0 + jnp.exp(-x))


def _silu(x):
    return x * _sigmoid(x)


def _gelu_tanh(x):
    c = np.float32(np.sqrt(2.0 / np.pi))
    return 0.5 * x * (1.0 + jnp.tanh(c * (x + 0.044715 * (x * x * x))))


def _rms(x, g):
    return x * lax.rsqrt(jnp.mean(x * x, axis=-1, keepdims=True) + EPS) * g


def _dot(a, b):
    return jnp.dot(a, b, preferred_element_type=F32)


def _dot_nt(a, b):
    return lax.dot_general(a, b, (((1,), (1,)), ((), ())), preferred_element_type=F32)


def _mod_kernel(c_ref, w_ref, b_ref, o_ref):
    s = _silu(c_ref[...]).astype(BF16)
    o_ref[...] = _dot(s, w_ref[...].astype(BF16)) + b_ref[...]


def _modulation(cond, w, b):
    tn = 1024
    n = w.shape[1]
    return pl.pallas_call(
        _mod_kernel,
        grid=(n // tn,),
        in_specs=[
            pl.BlockSpec((MOD_ROWS, D_MODEL), lambda j: (0, 0)),
            pl.BlockSpec((D_MODEL, tn), lambda j: (0, j)),
            pl.BlockSpec((1, tn), lambda j: (0, j)),
        ],
        out_specs=pl.BlockSpec((MOD_ROWS, tn), lambda j: (0, j)),
        out_shape=jax.ShapeDtypeStruct((MOD_ROWS, n), F32),
        compiler_params=_params("arbitrary"),
        name="modulation",
    )(cond, w, b.reshape(1, n))


def _mod_spec(which, tm, rows_per_mod, mod_off):
    return pl.BlockSpec(
        (None, None, 1, D_MODEL),
        lambda i, *_: (mod_off + (i * tm) // rows_per_mod, which, 0, 0))


def _norm_proj_kernel(x_ref, g_ref, sc_ref, sh_ref, w_ref, o_ref, h_scr):
    @pl.when(pl.program_id(1) == 0)
    def _():
        h = _rms(x_ref[...], g_ref[...]) * (1.0 + sc_ref[...]) + sh_ref[...]
        h_scr[...] = h.astype(BF16)

    o_ref[...] = _dot(h_scr[...], w_ref[...]).astype(o_ref.dtype)


def _norm_proj(x, g, mod4, sc_idx, sh_idx, w, rows_per_mod, mod_off, tm, tn, out_dtype):
    m = x.shape[0]
    n = w.shape[1]
    return pl.pallas_call(
        _norm_proj_kernel,
        grid=(m // tm, n // tn),
        in_specs=[
            pl.BlockSpec((tm, D_MODEL), lambda i, j: (i, 0)),
            pl.BlockSpec((1, D_MODEL), lambda i, j: (0, 0)),
            _mod_spec(sc_idx, tm, rows_per_mod, mod_off),
            _mod_spec(sh_idx, tm, rows_per_mod, mod_off),
            pl.BlockSpec((D_MODEL, tn), lambda i, j: (0, j)),
        ],
        out_specs=pl.BlockSpec((tm, tn), lambda i, j: (i, j)),
        out_shape=jax.ShapeDtypeStruct((m, n), out_dtype),
        scratch_shapes=[pltpu.VMEM((tm, D_MODEL), BF16)],
        compiler_params=_params("parallel", "arbitrary"),
        name="norm_proj",
    )(x, g.reshape(1, D_MODEL), mod4, mod4, w)


def _gla_kernel(qa_ref, fzf_ref, fzb_ref, ia_ref, ga_ref, lbp_ref, gn_ref, s0_ref,
                out_ref, *rest, n, emit_state):
    if emit_state:
        snew_ref, ops_scr, dec_scr, o_scr, st_scr = rest
    else:
        ops_scr, dec_scr, o_scr, st_scr = rest
    c_len = SCAN_CHUNK
    nc = n // c_len
    mid = c_len // 2
    rb = GLA_ROW_BLOCK
    cpb = rb // c_len

    def lower_bound(d):
        a = [lbp_ref[d, l] for l in range(lbp_ref.shape[1])]
        mx = functools.reduce(jnp.maximum, a)
        e = [jnp.exp(t - mx) for t in a]
        return e[0] / functools.reduce(lambda u, w: u + w, e)

    lbs = (lower_bound(0), lower_bound(1))
    pos = lax.broadcasted_iota(jnp.int32, (rb, A_DK), 0) % c_len

    def operands(i, carry):
        r0 = pl.multiple_of(i * rb, rb)
        rows = pl.ds(r0, rb)
        q = _silu(qa_ref[rows, :]) * (A_DK ** -0.5)
        for d, fz_ref in enumerate((fzf_ref, fzb_ref)):
            lb = lbs[d]
            f = lb + (1.0 - lb) * _sigmoid(fz_ref[rows, :])
            k = 1.0 - f
            b = jnp.log(f)
            for s in (1, 2, 4, 8, 16, 32):
                if d == 0:
                    b = b + jnp.where(pos >= s, pltpu.roll(b, s, axis=0), 0.0)
                else:
                    b = b + jnp.where(pos < c_len - s, pltpu.roll(b, rb - s, axis=0), 0.0)
            for ci in range(cpb):
                sl = slice(ci * c_len, (ci + 1) * c_len)
                bc = b[sl]
                if d == 0:
                    ref, b_last = bc[mid - 1:mid], bc[c_len - 1:c_len]
                else:
                    ref, b_last = bc[c_len - mid:c_len - mid + 1], bc[0:1]
                qe = q[sl] * jnp.exp(bc - ref)
                ke = k[sl] * jnp.exp(ref - bc)
                crow = pl.ds(r0 + ci * c_len, c_len)
                ops_scr[d, 0, crow, :] = qe.astype(BF16)
                ops_scr[d, 1, crow, :] = ke.astype(BF16)
                ops_scr[d, 2, crow, :] = (qe * jnp.exp(ref)).astype(BF16)
                ops_scr[d, 3, crow, :] = (ke * jnp.exp(b_last - ref)).astype(BF16)
                dec_scr[d, i * cpb + ci] = jnp.broadcast_to(jnp.exp(b_last), (8, A_DK))
        return carry

    lax.fori_loop(0, n // rb, operands, 0)

    ri = lax.broadcasted_iota(jnp.int32, (c_len, c_len), 0)
    ci_ = lax.broadcasted_iota(jnp.int32, (c_len, c_len), 1)
    keep = (ri >= ci_, ri <= ci_)
    for d in range(2):
        st_scr[d] = s0_ref[d].T

    def scan_step(c, carry):
        for d in range(2):
            cidx = c if d == 0 else nc - 1 - c
            rows = pl.ds(pl.multiple_of(cidx * c_len, c_len), c_len)
            vc = ia_ref[rows, :]
            scores = jnp.where(keep[d], _dot_nt(ops_scr[d, 0, rows, :], ops_scr[d, 1, rows, :]), 0.0)
            st = st_scr[d]
            o_scr[d, rows, :] = (_dot(scores.astype(BF16), vc.astype(BF16))
                                 + _dot_nt(ops_scr[d, 2, rows, :], st.astype(BF16)))
            st_scr[d] = dec_scr[d, cidx][0:1, :] * st + _dot(vc.T.astype(BF16), ops_scr[d, 3, rows, :])
        return carry

    lax.fori_loop(0, nc, scan_step, 0, unroll=True if nc <= 4 else 8)
    if emit_state:
        for d in range(2):
            snew_ref[d] = st_scr[d].T

    def finish(i, carry):
        rows = pl.ds(pl.multiple_of(i * rb, rb), rb)
        o = _rms(o_scr[0, rows, :] + o_scr[1, rows, :], gn_ref[...]) * _silu(ga_ref[rows, :])
        out_ref[rows, :] = o.astype(out_ref.dtype)
        return carry

    lax.fori_loop(0, n // rb, finish, 0)


def _gla(proj3, hgrn_lb, gnorm, s0, emit_state):
    bsz, n, _ = proj3.shape
    h = A_HEADS

    def col(k):
        return pl.BlockSpec((None, n, A_DK), lambda b, hh: (b, 0, k * h + hh))

    n_lb = hgrn_lb.shape[1]
    lb_spec = pl.BlockSpec((2, n_lb, None, 1, A_DK), lambda b, hh: (0, 0, hh, 0, 0))
    st_spec = pl.BlockSpec((None, 2, None, A_DK, A_DV), lambda b, hh: (b, 0, hh, 0, 0))
    out_shape = [jax.ShapeDtypeStruct((bsz, n, A_WIDTH), BF16)]
    out_specs = [pl.BlockSpec((None, n, A_DV), lambda b, hh: (b, 0, hh))]
    if emit_state:
        out_shape.append(jax.ShapeDtypeStruct((bsz, 2, h, A_DK, A_DV), F32))
        out_specs.append(st_spec)
    res = pl.pallas_call(
        functools.partial(_gla_kernel, n=n, emit_state=emit_state),
        grid=(bsz, h),
        in_specs=[col(0), col(1), col(2), col(3), col(4), lb_spec,
                  pl.BlockSpec((1, A_DV), lambda b, hh: (0, 0)), st_spec],
        out_specs=out_specs,
        out_shape=out_shape,
        scratch_shapes=[pltpu.VMEM((2, 4, n, A_DK), BF16),
                        pltpu.VMEM((2, n // SCAN_CHUNK, 8, A_DK), F32),
                        pltpu.VMEM((2, n, A_DV), F32),
                        pltpu.VMEM((2, A_DV, A_DK), F32)],
        compiler_params=_params("parallel", "parallel"),
        name="hgrn2",
    )(proj3, proj3, proj3, proj3, proj3, hgrn_lb.reshape(2, n_lb, h, 1, A_DK),
      gnorm.reshape(1, A_DV), s0)
    return res


def _gmlp_kernel(ub_ref, vb_ref, vn_ref, ws_ref, bst_ref, out_ref, *, rows):
    for g in range(B_GROUPS):
        cs = slice(g * B_CG, (g + 1) * B_CG)
        vv = _rms(_gelu_tanh(vb_ref[:, cs]), vn_ref[:, cs]).astype(BF16)
        w = ws_ref[g].astype(BF16)
        bias = bst_ref[:, g:g + 1]
        for c in range(rows // B_CHUNK):
            rs = slice(c * B_CHUNK, (c + 1) * B_CHUNK)
            mixed = _dot(w, vv[rs, :]) + bias
            out_ref[rs, cs] = (_gelu_tanh(ub_ref[rs, cs]) * mixed).astype(out_ref.dtype)


def _gmlp(proj, vnorm, ws, bs, rows):
    m = proj.shape[0]
    ub_blk = 5 * A_WIDTH // B_WIDTH
    return pl.pallas_call(
        functools.partial(_gmlp_kernel, rows=rows),
        grid=(m // rows,),
        in_specs=[
            pl.BlockSpec((rows, B_WIDTH), lambda i: (i, ub_blk)),
            pl.BlockSpec((rows, B_WIDTH), lambda i: (i, ub_blk + 1)),
            pl.BlockSpec((1, B_WIDTH), lambda i: (0, 0)),
            pl.BlockSpec((B_GROUPS, B_CHUNK, B_CHUNK), lambda i: (0, 0, 0)),
            pl.BlockSpec((B_CHUNK, B_GROUPS), lambda i: (0, 0)),
        ],
        out_specs=pl.BlockSpec((rows, B_WIDTH), lambda i: (i, 0)),
        out_shape=jax.ShapeDtypeStruct((m, B_WIDTH), BF16),
        compiler_params=_params("parallel"),
        name="gmlp",
    )(proj, proj, vnorm.reshape(1, B_WIDTH), ws, bs.T)


def _resid_proj_kernel(*refs, n_parts):
    x_ref, gate_ref = refs[0], refs[1]
    a_refs = refs[2:2 + n_parts]
    w_refs = refs[2 + n_parts:2 + 2 * n_parts]
    o_ref = refs[2 + 2 * n_parts]
    acc = _dot(a_refs[0][...], w_refs[0][...])
    for a_ref, w_ref in zip(a_refs[1:], w_refs[1:]):
        acc = acc + _dot(a_ref[...], w_ref[...])
    o_ref[...] = x_ref[...] + gate_ref[...] * acc


def _resid_proj(x, mod4, gate_idx, parts, w, rows_per_mod, mod_off, tm):
    m = x.shape[0]
    n_parts = len(parts)
    kp = parts[0].shape[1]
    in_specs = [pl.BlockSpec((tm, D_MODEL), lambda i: (i, 0)),
                _mod_spec(gate_idx, tm, rows_per_mod, mod_off)]
    in_specs += [pl.BlockSpec((tm, kp), lambda i: (i, 0)) for _ in parts]
    in_specs += [pl.BlockSpec((kp, D_MODEL), lambda i, p=p: (p, 0)) for p in range(n_parts)]
    return pl.pallas_call(
        functools.partial(_resid_proj_kernel, n_parts=n_parts),
        grid=(m // tm,),
        in_specs=in_specs,
        out_specs=pl.BlockSpec((tm, D_MODEL), lambda i: (i, 0)),
        out_shape=jax.ShapeDtypeStruct((m, D_MODEL), F32),
        compiler_params=_params("parallel"),
        name="resid_proj",
    )(x, mod4, *parts, *([w] * n_parts))


def _dft_tables(n):
    idx = np.arange(n, dtype=np.int64)
    ang = 2.0 * np.pi * ((idx[:, None] * idx[None, :]) % n).astype(np.float64) / n
    s = 1.0 / np.sqrt(n)
    return (np.cos(ang) * s).astype(np.float32), (np.sin(ang) * s).astype(np.float32)


def _norm_chan_dft_kernel(x_ref, g_ref, sc_ref, sh_ref, cc_ref, sn_ref, p_ref, q_ref):
    h = (_rms(x_ref[...], g_ref[...]) * (1.0 + sc_ref[...]) + sh_ref[...]).astype(BF16)
    for g in range(C_GROUPS):
        cs = slice(g * C_CG, (g + 1) * C_CG)
        p_ref[:, cs] = _dot(h[:, cs], cc_ref[...]).astype(p_ref.dtype)
        q_ref[:, cs] = _dot(h[:, cs], sn_ref[...]).astype(q_ref.dtype)


def _norm_chan_dft(x, g, mod4, sc_idx, sh_idx, rows_per_mod, mod_off, tm):
    m = x.shape[0]
    cc, sn = _dft_tables(C_CG)
    cc = jnp.asarray(cc).astype(BF16)
    sn = jnp.asarray(sn).astype(BF16)
    tab = pl.BlockSpec((C_CG, C_CG), lambda i: (0, 0))
    out = pl.BlockSpec((tm, D_MODEL), lambda i: (i, 0))
    return pl.pallas_call(
        _norm_chan_dft_kernel,
        grid=(m // tm,),
        in_specs=[
            pl.BlockSpec((tm, D_MODEL), lambda i: (i, 0)),
            pl.BlockSpec((1, D_MODEL), lambda i: (0, 0)),
            _mod_spec(sc_idx, tm, rows_per_mod, mod_off),
            _mod_spec(sh_idx, tm, rows_per_mod, mod_off),
            tab, tab,
        ],
        out_specs=[out, out],
        out_shape=[jax.ShapeDtypeStruct((m, D_MODEL), BF16)] * 2,
        compiler_params=_params("parallel"),
        name="norm_chan_dft",
    )(x, g.reshape(1, D_MODEL), mod4, mod4, cc, sn)


def _pos_dft_kernel(cn_ref, sn_ref, p_ref, q_ref, o_ref):
    o_ref[...] = (_dot(cn_ref[...], p_ref[...]) - _dot(sn_ref[...], q_ref[...])).astype(o_ref.dtype)


def _pos_dft(p3, q3, tn):
    bsz, n, _ = p3.shape
    cn, sn = _dft_tables(n)
    cn = jnp.asarray(cn).astype(BF16)
    sn = jnp.asarray(sn).astype(BF16)
    tab = pl.BlockSpec((n, n), lambda b, j: (0, 0))
    blk = pl.BlockSpec((None, n, tn), lambda b, j: (b, 0, j))
    return pl.pallas_call(
        _pos_dft_kernel,
        grid=(bsz, D_MODEL // tn),
        in_specs=[tab, tab, blk, blk],
        out_specs=blk,
        out_shape=jax.ShapeDtypeStruct((bsz, n, D_MODEL), BF16),
        compiler_params=_params("parallel", "parallel"),
        name="pos_dft",
    )(cn, sn, p3, q3)


FFN_SCHED = {name: i for i, name in enumerate((
    "up_rows", "up_mod", "up_col", "conv_col", "down_rows", "down_mod", "down_col",
    "norm_due", "reset_due", "out_due"))}


def _ffn_kernel(sched_ref, xc_ref, g_ref, sc_ref, sh_ref, xp_ref, gate_ref, wup_ref, cp_ref, wd_ref,
                mask_ref, fin_ref, o_ref, h_scr, acc_scr, u0_scr, u1_scr, act0_scr, act1_scr, *,
                final_norm, pieces):
    t = pl.program_id(0)
    tm, tf = act0_scr.shape
    lanes = mask_ref.shape[2]

    def due(name):
        return sched_ref[FFN_SCHED[name], t] != 0

    @pl.when(t == 0)
    def _():
        for ref in (u0_scr, u1_scr, act0_scr, act1_scr, acc_scr):
            ref[...] = jnp.zeros_like(ref)

    @pl.when(due("norm_due"))
    def _():
        h = _rms(xc_ref[...], g_ref[...]) * (1.0 + sc_ref[...]) + sh_ref[...]
        h_scr[...] = h.astype(BF16)

    @pl.when(due("reset_due"))
    def _():
        acc_scr[...] = jnp.zeros_like(acc_scr)

    cb_rows = FFN_CONV_ROWS
    n_row_blocks = tm // cb_rows
    mask_rows = mask_ref.shape[1]

    def conv(u_old, r, cols, cs, p0):
        ext = u_old[r * cb_rows:(r + 1) * cb_rows + 2 * HALO, cols]
        n_ext = cb_rows + 2 * HALO
        ms = slice((r * cb_rows) % mask_rows, (r * cb_rows) % mask_rows + cb_rows)
        inner = slice(HALO, HALO + cb_rows)
        prev = pltpu.roll(ext, 1, axis=0)[inner] * mask_ref[0, ms, :]
        nxt = pltpu.roll(ext, n_ext - 1, axis=0)[inner] * mask_ref[1, ms, :]
        return (prev * cp_ref[p0:p0 + 1, cs] + ext[inner] * cp_ref[p0 + 1:p0 + 2, cs]
                + nxt * cp_ref[p0 + 2:p0 + 3, cs] + cp_ref[p0 + 3:p0 + 4, cs])

    def conv_block(u_old, act_new, idx):
        c, r = divmod(idx, n_row_blocks)
        cs = slice(c * lanes, (c + 1) * lanes)
        gate = conv(u_old, r, cs, cs, 0)
        val = conv(u_old, r, slice(tf + c * lanes, tf + (c + 1) * lanes), cs, 4)
        act_new[r * cb_rows:(r + 1) * cb_rows, cs] = (_silu(gate) * val).astype(BF16)

    def stages(u_new, u_old, act_new, act_old):
        n_blocks = (tf // lanes) * n_row_blocks
        dn = D_MODEL // pieces
        un = 2 * tf // pieces
        h = h_scr[...]
        act = act_old[...]
        work = [1] * pieces + [2] * pieces
        early = n_blocks - FFN_TAIL_BLOCKS
        total = sum(work[:-1])
        done = 0
        for piece in range(2 * pieces):
            if piece < pieces:
                ns = slice(piece * dn, (piece + 1) * dn)
                acc_scr[:, ns] += _dot(act, wd_ref[:, ns])
            else:
                q = piece - pieces
                ns = slice(q * un, (q + 1) * un)
                u_new[HALO:HALO + tm, ns] = _dot(h, wup_ref[:, ns])
            if piece == 2 * pieces - 1:
                target = n_blocks
            else:
                target = min(early, -(-early * sum(work[:piece + 1]) // total))
            for idx in range(done, target):
                conv_block(u_old, act_new, idx)
            done = target

    @pl.when(t % 2 == 0)
    def _():
        stages(u0_scr, u1_scr, act1_scr, act0_scr)

    @pl.when(t % 2 == 1)
    def _():
        stages(u1_scr, u0_scr, act0_scr, act1_scr)

    @pl.when(due("out_due"))
    def _():
        y = xp_ref[...] + gate_ref[...] * acc_scr[...]
        if final_norm:
            y = _rms(y, fin_ref[...])
        o_ref[...] = y


def _tile_ffn_params(w_up, cw, cb, tf):
    nf = D_FF // tf
    w = w_up.astype(BF16)
    w_t = jnp.concatenate([w[:, :D_FF].reshape(D_MODEL, nf, tf), w[:, D_FF:].reshape(D_MODEL, nf, tf)],
                          axis=2).transpose(1, 0, 2)
    cp = jnp.concatenate([cw[:, :D_FF], cb[None, :D_FF], cw[:, D_FF:], cb[None, D_FF:]], axis=0)
    return w_t, cp.reshape(8, nf, tf).transpose(1, 0, 2)


def _conv_ffn(x, g, mod4, w_up_t, conv_p, w_down, fin, rows_per_mod, mod_off, row_len, tm,
              final_norm, pieces):
    m = x.shape[0]
    nf, _, tf2 = w_up_t.shape
    tf = tf2 // 2
    n_tiles = (m // tm) * nf
    rb = max(row_len, FFN_CONV_ROWS)
    assert rb % row_len == 0 and rb % FFN_CONV_ROWS == 0 and tm % rb == 0
    pos = np.arange(rb) % row_len
    mask = np.stack([np.broadcast_to((pos != 0)[:, None], (rb, 128)),
                     np.broadcast_to((pos != row_len - 1)[:, None], (rb, 128))]).astype(np.float32)

    t = np.arange(n_tiles + 2)
    up = np.minimum(t, n_tiles - 1)
    cv = np.clip(t - 1, 0, n_tiles - 1)
    dn = np.clip(t - 2, 0, n_tiles - 1)
    sched = np.zeros((len(FFN_SCHED), n_tiles + 2), np.int32)
    sched[FFN_SCHED["up_rows"]] = up // nf
    sched[FFN_SCHED["up_mod"]] = mod_off + (up // nf * tm) // rows_per_mod
    sched[FFN_SCHED["up_col"]] = up % nf
    sched[FFN_SCHED["conv_col"]] = cv % nf
    sched[FFN_SCHED["down_rows"]] = dn // nf
    sched[FFN_SCHED["down_mod"]] = mod_off + (dn // nf * tm) // rows_per_mod
    sched[FFN_SCHED["down_col"]] = dn % nf
    sched[FFN_SCHED["norm_due"]] = (t < n_tiles) & (t % nf == 0)
    sched[FFN_SCHED["reset_due"]] = (t >= 2) & ((t - 2) % nf == 0)
    sched[FFN_SCHED["out_due"]] = (t >= 2) & ((t - 2) % nf == nf - 1)

    def row(name):
        return FFN_SCHED[name]

    def mod_spec(which, name):
        return pl.BlockSpec((None, None, 1, D_MODEL), lambda t, s: (s[row(name), t], which, 0, 0))

    vec = pl.BlockSpec((1, D_MODEL), lambda t, s: (0, 0))
    grid_spec = pltpu.PrefetchScalarGridSpec(
        num_scalar_prefetch=1,
        grid=(n_tiles + 2,),
        in_specs=[
            pl.BlockSpec((tm, D_MODEL), lambda t, s: (s[row("up_rows"), t], 0)),
            vec,
            mod_spec(4, "up_mod"),
            mod_spec(3, "up_mod"),
            pl.BlockSpec((tm, D_MODEL), lambda t, s: (s[row("down_rows"), t], 0)),
            mod_spec(5, "down_mod"),
            pl.BlockSpec((None, D_MODEL, 2 * tf), lambda t, s: (s[row("up_col"), t], 0, 0)),
            pl.BlockSpec((None, 8, tf), lambda t, s: (s[row("conv_col"), t], 0, 0)),
            pl.BlockSpec((tf, D_MODEL), lambda t, s: (s[row("down_col"), t], 0)),
            pl.BlockSpec((2, rb, 128), lambda t, s: (0, 0, 0)),
            vec,
        ],
        out_specs=pl.BlockSpec((tm, D_MODEL), lambda t, s: (s[row("down_rows"), t], 0)),
        scratch_shapes=[pltpu.VMEM((tm, D_MODEL), BF16), pltpu.VMEM((tm, D_MODEL), F32),
                        pltpu.VMEM((tm + 2 * HALO, 2 * tf), F32),
                        pltpu.VMEM((tm + 2 * HALO, 2 * tf), F32),
                        pltpu.VMEM((tm, tf), BF16), pltpu.VMEM((tm, tf), BF16)],
    )
    return pl.pallas_call(
        functools.partial(_ffn_kernel, final_norm=final_norm, pieces=pieces),
        grid_spec=grid_spec,
        out_shape=jax.ShapeDtypeStruct((m, D_MODEL), F32),
        compiler_params=_params("arbitrary"),
        name="conv_ffn",
    )(jnp.asarray(sched), x, g.reshape(1, D_MODEL), mod4, mod4, x, mod4, w_up_t, conv_p, w_down,
      jnp.asarray(mask), fin.reshape(1, D_MODEL))


def kernel(x_prompt, x_sample, state_l0_hgrn, c, c_ctx, mod_w_0, mod_b_0, norm1_0, w_in_0, hgrn_lb, hgrn_gnorm_0, gmlp_vnorm_0, gmlp_ws_0, gmlp_bs_0, w_out_0, norm2_0, ffn_up_0, ffn_conv_w_0, ffn_conv_b_0, ffn_down_0, mod_w_1, mod_b_1, norm1_1, w_out_1, norm2_1, ffn_up_1, ffn_conv_w_1, ffn_conv_b_1, ffn_down_1, final_norm):
    n_dec = c.shape[0]
    cond = jnp.concatenate(
        [c, c_ctx[None, :], jnp.zeros((MOD_ROWS - n_dec - 1, D_MODEL), F32)], axis=0)
    mod4 = [_modulation(cond, w, b).reshape(MOD_ROWS, 6, 1, D_MODEL)
            for w, b in ((mod_w_0, mod_b_0), (mod_w_1, mod_b_1))]

    w_in = w_in_0.astype(BF16)
    w_out0 = w_out_0.astype(BF16)
    w_out1 = w_out_1.astype(BF16)
    ffn = [(norm2_0, *_tile_ffn_params(ffn_up_0, ffn_conv_w_0, ffn_conv_b_0, FFN_TF), ffn_down_0.astype(BF16)),
           (norm2_1, *_tile_ffn_params(ffn_up_1, ffn_conv_w_1, ffn_conv_b_1, FFN_TF), ffn_down_1.astype(BF16))]

    def trunk(x3, s0, mod_off, per_batch_mod, row_len, emit_state):
        bsz, n, _ = x3.shape
        m = bsz * n
        rows_per_mod = n if per_batch_mod else m
        tm = ROW_TILE
        x = x3.reshape(m, D_MODEL)

        proj = _norm_proj(x, norm1_0, mod4[0], 1, 0, w_in, rows_per_mod, mod_off, PROJ_ROW_TILE,
                          PROJ_COL_TILE, F32)
        res = _gla(proj.reshape(bsz, n, IN_WIDTH), hgrn_lb, hgrn_gnorm_0, s0,
                   emit_state)
        out_b = _gmlp(proj, gmlp_vnorm_0, gmlp_ws_0, gmlp_bs_0, ROW_TILE)
        x = _resid_proj(x, mod4[0], 2, [res[0].reshape(m, A_WIDTH), out_b], w_out0,
                        rows_per_mod, mod_off, tm)
        x = _conv_ffn(x, ffn[0][0], mod4[0], *ffn[0][1:], final_norm, rows_per_mod, mod_off,
                      row_len, tm, False, FFN_MXU_PIECES[0])

        p, q = _norm_chan_dft(x, norm1_1, mod4[1], 1, 0, rows_per_mod, mod_off, tm)
        four = _pos_dft(p.reshape(bsz, n, D_MODEL), q.reshape(bsz, n, D_MODEL), POS_DFT_COL_TILE)
        x = _resid_proj(x, mod4[1], 2, [four.reshape(m, D_MODEL)], w_out1, rows_per_mod, mod_off, tm)
        x = _conv_ffn(x, ffn[1][0], mod4[1], *ffn[1][1:], final_norm, rows_per_mod, mod_off,
                      row_len, tm, True, FFN_MXU_PIECES[1])
        return x.reshape(bsz, n, D_MODEL), (res[1] if emit_state else None)

    zero_state = jnp.zeros((x_prompt.shape[0], 2, A_HEADS, A_DK, A_DV), F32)
    y_prompt, state_new = trunk(x_prompt, zero_state, n_dec, False, x_prompt.shape[1], True)
    y_sample, _ = trunk(x_sample, state_l0_hgrn, 0, True, GRID_W, False)
    return (y_prompt, y_sample, state_new.astype(x_prompt.dtype))
```

```python
import functools

import numpy as np
import jax
import jax.numpy as jnp
from jax import lax
from jax.experimental import pallas as pl
from jax.experimental.pallas import tpu as pltpu

D_MODEL = 2048
A_WIDTH = 1024
A_HEADS = 8
A_DK = 128
A_DV = 128
B_WIDTH = 1024
B_GROUPS = 4
B_CHUNK = 128
B_CG = B_WIDTH // B_GROUPS
C_GROUPS = 4
C_CG = D_MODEL // C_GROUPS
SCAN_CHUNK = 64
D_FF = 5632
GRID_W = 64
IN_WIDTH = 5 * A_WIDTH + 2 * B_WIDTH
EPS = 1e-6

F32 = jnp.float32
BF16 = jnp.bfloat16

VMEM_LIMIT = 56 * 1024 * 1024
MOD_ROWS = 16
GLA_ROW_BLOCK = 256
FFN_CONV_ROWS = 64
FFN_MXU_PIECES = (4, 4)
FFN_TAIL_BLOCKS = 3
HALO = 8
ROW_TILE = 512
FFN_TF = 512
PROJ_ROW_TILE = 1024
PROJ_COL_TILE = 1024
POS_DFT_COL_TILE = 512


def _params(*sem):
    return pltpu.CompilerParams(dimension_semantics=sem, vmem_limit_bytes=VMEM_LIMIT)


def _sigmoid(x):
    return 1.0 / (1.0 + jnp.exp(-x))


def _silu(x):
    return x * _sigmoid(x)


def _gelu_tanh(x):
    c = np.float32(np.sqrt(2.0 / np.pi))
    return 0.5 * x * (1.0 + jnp.tanh(c * (x + 0.044715 * (x * x * x))))


def _rms(x, g):
    return x * lax.rsqrt(jnp.mean(x * x, axis=-1, keepdims=True) + EPS) * g


def _dot(a, b):
    return jnp.dot(a, b, preferred_element_type=F32)


def _dot_nt(a, b):
    return lax.dot_general(a, b, (((1,), (1,)), ((), ())), preferred_element_type=F32)


def _mod_kernel(c_ref, w_ref, b_ref, o_ref):
    s = _silu(c_ref[...]).astype(BF16)
    o_ref[...] = _dot(s, w_ref[...].astype(BF16)) + b_ref[...]


def _modulation(cond, w, b):
    tn = 1024
    n = w.shape[1]
    return pl.pallas_call(
        _mod_kernel,
        grid=(n // tn,),
        in_specs=[
            pl.BlockSpec((MOD_ROWS, D_MODEL), lambda j: (0, 0)),
            pl.BlockSpec((D_MODEL, tn), lambda j: (0, j)),
            pl.BlockSpec((1, tn), lambda j: (0, j)),
        ],
        out_specs=pl.BlockSpec((MOD_ROWS, tn), lambda j: (0, j)),
        out_shape=jax.ShapeDtypeStruct((MOD_ROWS, n), F32),
        compiler_params=_params("arbitrary"),
        name="modulation",
    )(cond, w, b.reshape(1, n))


def _mod_spec(which, tm, rows_per_mod, mod_off):
    return pl.BlockSpec(
        (None, None, 1, D_MODEL),
        lambda i, *_: (mod_off + (i * tm) // rows_per_mod, which, 0, 0))


def _norm_proj_kernel(x_ref, g_ref, sc_ref, sh_ref, w_ref, o_ref, h_scr):
    @pl.when(pl.program_id(1) == 0)
    def _():
        h = _rms(x_ref[...], g_ref[...]) * (1.0 + sc_ref[...]) + sh_ref[...]
        h_scr[...] = h.astype(BF16)

    o_ref[...] = _dot(h_scr[...], w_ref[...]).astype(o_ref.dtype)


def _norm_proj(x, g, mod4, sc_idx, sh_idx, w, rows_per_mod, mod_off, tm, tn, out_dtype):
    m = x.shape[0]
    n = w.shape[1]
    return pl.pallas_call(
        _norm_proj_kernel,
        grid=(m // tm, n // tn),
        in_specs=[
            pl.BlockSpec((tm, D_MODEL), lambda i, j: (i, 0)),
            pl.BlockSpec((1, D_MODEL), lambda i, j: (0, 0)),
            _mod_spec(sc_idx, tm, rows_per_mod, mod_off),
            _mod_spec(sh_idx, tm, rows_per_mod, mod_off),
            pl.BlockSpec((D_MODEL, tn), lambda i, j: (0, j)),
        ],
        out_specs=pl.BlockSpec((tm, tn), lambda i, j: (i, j)),
        out_shape=jax.ShapeDtypeStruct((m, n), out_dtype),
        scratch_shapes=[pltpu.VMEM((tm, D_MODEL), BF16)],
        compiler_params=_params("parallel", "arbitrary"),
        name="norm_proj",
    )(x, g.reshape(1, D_MODEL), mod4, mod4, w)


def _gla_kernel(qa_ref, fzf_ref, fzb_ref, ia_ref, ga_ref, lbp_ref, gn_ref, s0_ref,
                out_ref, *rest, n, emit_state):
    if emit_state:
        snew_ref, ops_scr, dec_scr, p_scr, kv_scr, o_scr, st_scr = rest
    else:
        ops_scr, dec_scr, p_scr, kv_scr, o_scr, st_scr = rest
    c_len = SCAN_CHUNK
    nc = n // c_len
    mid = c_len // 2
    rb = GLA_ROW_BLOCK
    cpb = rb // c_len

    def lower_bound(d):
        a = [lbp_ref[d, l] for l in range(lbp_ref.shape[1])]
        mx = functools.reduce(jnp.maximum, a)
        e = [jnp.exp(t - mx) for t in a]
        return e[0] / functools.reduce(lambda u, w: u + w, e)

    lbs = (lower_bound(0), lower_bound(1))
    pos = lax.broadcasted_iota(jnp.int32, (rb, A_DK), 0) % c_len

    def operands(i, carry):
        r0 = pl.multiple_of(i * rb, rb)
        rows = pl.ds(r0, rb)
        q = _silu(qa_ref[rows, :]) * (A_DK ** -0.5)
        for d, fz_ref in enumerate((fzf_ref, fzb_ref)):
            lb = lbs[d]
            f = lb + (1.0 - lb) * _sigmoid(fz_ref[rows, :])
            k = 1.0 - f
            b = jnp.log2(f)
            for s in (1, 2, 4, 8, 16, 32):
                if d == 0:
                    b = b + jnp.where(pos >= s, pltpu.roll(b, s, axis=0), 0.0)
                else:
                    b = b + jnp.where(pos < c_len - s, pltpu.roll(b, rb - s, axis=0), 0.0)
            for ci in range(cpb):
                sl = slice(ci * c_len, (ci + 1) * c_len)
                bc = b[sl]
                if d == 0:
                    ref, b_last = bc[mid - 1:mid], bc[c_len - 1:c_len]
                else:
                    ref, b_last = bc[c_len - mid:c_len - mid + 1], bc[0:1]
                qe = q[sl] * jnp.exp2(bc - ref)
                ke = k[sl] * jnp.exp2(ref - bc)
                crow = pl.ds(r0 + ci * c_len, c_len)
                ops_scr[d, 0, crow, :] = qe.astype(BF16)
                ops_scr[d, 1, crow, :] = ke.astype(BF16)
                ops_scr[d, 2, crow, :] = (qe * jnp.exp2(ref)).astype(BF16)
                ops_scr[d, 3, crow, :] = (ke * jnp.exp2(b_last - ref)).astype(BF16)
                dec_scr[d, i * cpb + ci] = jnp.broadcast_to(jnp.exp2(b_last), (8, A_DK))
        return carry

    lax.fori_loop(0, n // rb, operands, 0)

    ri = lax.broadcasted_iota(jnp.int32, (c_len, c_len), 0)
    ci_ = lax.broadcasted_iota(jnp.int32, (c_len, c_len), 1)
    keep = (ri >= ci_, ri <= ci_)
    unroll = True if nc <= 4 else 8

    def chunk_products(c, carry):
        rows = pl.ds(pl.multiple_of(c * c_len, c_len), c_len)
        vt = ia_ref[rows, :].T.astype(BF16)
        for d in range(2):
            scores = jnp.where(keep[d], _dot_nt(ops_scr[d, 0, rows, :], ops_scr[d, 1, rows, :]), 0.0)
            p_scr[d, rows, :] = scores.astype(BF16)
            kv_scr[d, c] = _dot(vt, ops_scr[d, 3, rows, :])
        return carry

    lax.fori_loop(0, nc, chunk_products, 0, unroll=unroll)

    for d in range(2):
        st_scr[d] = s0_ref[d].T

    def scan_step(c, carry):
        for d in range(2):
            cidx = c if d == 0 else nc - 1 - c
            rows = pl.ds(pl.multiple_of(cidx * c_len, c_len), c_len)
            st = st_scr[d]
            o_scr[d, rows, :] = (_dot(p_scr[d, rows, :], ia_ref[rows, :].astype(BF16))
                                 + _dot_nt(ops_scr[d, 2, rows, :], st.astype(BF16)))
            st_scr[d] = dec_scr[d, cidx][0:1, :] * st + kv_scr[d, cidx]
        return carry

    lax.fori_loop(0, nc, scan_step, 0, unroll=unroll)
    if emit_state:
        for d in range(2):
            snew_ref[d] = st_scr[d].T

    def finish(i, carry):
        rows = pl.ds(pl.multiple_of(i * rb, rb), rb)
        o = _rms(o_scr[0, rows, :] + o_scr[1, rows, :], gn_ref[...]) * _silu(ga_ref[rows, :])
        out_ref[rows, :] = o.astype(out_ref.dtype)
        return carry

    lax.fori_loop(0, n // rb, finish, 0)


def _gla(proj3, hgrn_lb, gnorm, s0, emit_state):
    bsz, n, _ = proj3.shape
    h = A_HEADS

    def col(k):
        return pl.BlockSpec((None, n, A_DK), lambda b, hh: (b, 0, k * h + hh))

    n_lb = hgrn_lb.shape[1]
    lb_spec = pl.BlockSpec((2, n_lb, None, 1, A_DK), lambda b, hh: (0, 0, hh, 0, 0))
    st_spec = pl.BlockSpec((None, 2, None, A_DK, A_DV), lambda b, hh: (b, 0, hh, 0, 0))
    out_shape = [jax.ShapeDtypeStruct((bsz, n, A_WIDTH), BF16)]
    out_specs = [pl.BlockSpec((None, n, A_DV), lambda b, hh: (b, 0, hh))]
    if emit_state:
        out_shape.append(jax.ShapeDtypeStruct((bsz, 2, h, A_DK, A_DV), F32))
        out_specs.append(st_spec)
    res = pl.pallas_call(
        functools.partial(_gla_kernel, n=n, emit_state=emit_state),
        grid=(bsz, h),
        in_specs=[col(0), col(1), col(2), col(3), col(4), lb_spec,
                  pl.BlockSpec((1, A_DV), lambda b, hh: (0, 0)), st_spec],
        out_specs=out_specs,
        out_shape=out_shape,
        scratch_shapes=[pltpu.VMEM((2, 4, n, A_DK), BF16),
                        pltpu.VMEM((2, n // SCAN_CHUNK, 8, A_DK), F32),
                        pltpu.VMEM((2, n, SCAN_CHUNK), BF16),
                        pltpu.VMEM((2, n // SCAN_CHUNK, A_DV, A_DK), F32),
                        pltpu.VMEM((2, n, A_DV), F32),
                        pltpu.VMEM((2, A_DV, A_DK), F32)],
        compiler_params=_params("parallel", "parallel"),
        name="hgrn2",
    )(proj3, proj3, proj3, proj3, proj3, hgrn_lb.reshape(2, n_lb, h, 1, A_DK),
      gnorm.reshape(1, A_DV), s0)
    return res


def _gmlp_kernel(ub_ref, vb_ref, vn_ref, ws_ref, bst_ref, out_ref, *, rows):
    for g in range(B_GROUPS):
        cs = slice(g * B_CG, (g + 1) * B_CG)
        vv = _rms(_gelu_tanh(vb_ref[:, cs]), vn_ref[:, cs]).astype(BF16)
        w = ws_ref[g].astype(BF16)
        bias = bst_ref[:, g:g + 1]
        for c in range(rows // B_CHUNK):
            rs = slice(c * B_CHUNK, (c + 1) * B_CHUNK)
            mixed = _dot(w, vv[rs, :]) + bias
            out_ref[rs, cs] = (_gelu_tanh(ub_ref[rs, cs]) * mixed).astype(out_ref.dtype)


def _gmlp(proj, vnorm, ws, bs, rows):
    m = proj.shape[0]
    ub_blk = 5 * A_WIDTH // B_WIDTH
    return pl.pallas_call(
        functools.partial(_gmlp_kernel, rows=rows),
        grid=(m // rows,),
        in_specs=[
            pl.BlockSpec((rows, B_WIDTH), lambda i: (i, ub_blk)),
            pl.BlockSpec((rows, B_WIDTH), lambda i: (i, ub_blk + 1)),
            pl.BlockSpec((1, B_WIDTH), lambda i: (0, 0)),
            pl.BlockSpec((B_GROUPS, B_CHUNK, B_CHUNK), lambda i: (0, 0, 0)),
            pl.BlockSpec((B_CHUNK, B_GROUPS), lambda i: (0, 0)),
        ],
        out_specs=pl.BlockSpec((rows, B_WIDTH), lambda i: (i, 0)),
        out_shape=jax.ShapeDtypeStruct((m, B_WIDTH), BF16),
        compiler_params=_params("parallel"),
        name="gmlp",
    )(proj, proj, vnorm.reshape(1, B_WIDTH), ws, bs.T)


def _resid_proj_kernel(*refs, n_parts):
    x_ref, gate_ref = refs[0], refs[1]
    a_refs = refs[2:2 + n_parts]
    w_refs = refs[2 + n_parts:2 + 2 * n_parts]
    o_ref = refs[2 + 2 * n_parts]
    acc = _dot(a_refs[0][...], w_refs[0][...])
    for a_ref, w_ref in zip(a_refs[1:], w_refs[1:]):
        acc = acc + _dot(a_ref[...], w_ref[...])
    o_ref[...] = x_ref[...] + gate_ref[...] * acc


def _resid_proj(x, mod4, gate_idx, parts, w, rows_per_mod, mod_off, tm):
    m = x.shape[0]
    n_parts = len(parts)
    kp = parts[0].shape[1]
    in_specs = [pl.BlockSpec((tm, D_MODEL), lambda i: (i, 0)),
                _mod_spec(gate_idx, tm, rows_per_mod, mod_off)]
    in_specs += [pl.BlockSpec((tm, kp), lambda i: (i, 0)) for _ in parts]
    in_specs += [pl.BlockSpec((kp, D_MODEL), lambda i, p=p: (p, 0)) for p in range(n_parts)]
    return pl.pallas_call(
        functools.partial(_resid_proj_kernel, n_parts=n_parts),
        grid=(m // tm,),
        in_specs=in_specs,
        out_specs=pl.BlockSpec((tm, D_MODEL), lambda i: (i, 0)),
        out_shape=jax.ShapeDtypeStruct((m, D_MODEL), F32),
        compiler_params=_params("parallel"),
        name="resid_proj",
    )(x, mod4, *parts, *([w] * n_parts))


def _dft_tables(n):
    idx = np.arange(n, dtype=np.int64)
    ang = 2.0 * np.pi * ((idx[:, None] * idx[None, :]) % n).astype(np.float64) / n
    s = 1.0 / np.sqrt(n)
    return (np.cos(ang) * s).astype(np.float32), (np.sin(ang) * s).astype(np.float32)


def _norm_chan_dft_kernel(x_ref, g_ref, sc_ref, sh_ref, cc_ref, sn_ref, p_ref, q_ref):
    h = (_rms(x_ref[...], g_ref[...]) * (1.0 + sc_ref[...]) + sh_ref[...]).astype(BF16)
    for g in range(C_GROUPS):
        cs = slice(g * C_CG, (g + 1) * C_CG)
        p_ref[:, cs] = _dot(h[:, cs], cc_ref[...]).astype(p_ref.dtype)
        q_ref[:, cs] = _dot(h[:, cs], sn_ref[...]).astype(q_ref.dtype)


def _norm_chan_dft(x, g, mod4, sc_idx, sh_idx, rows_per_mod, mod_off, tm):
    m = x.shape[0]
    cc, sn = _dft_tables(C_CG)
    cc = jnp.asarray(cc).astype(BF16)
    sn = jnp.asarray(sn).astype(BF16)
    tab = pl.BlockSpec((C_CG, C_CG), lambda i: (0, 0))
    out = pl.BlockSpec((tm, D_MODEL), lambda i: (i, 0))
    return pl.pallas_call(
        _norm_chan_dft_kernel,
        grid=(m // tm,),
        in_specs=[
            pl.BlockSpec((tm, D_MODEL), lambda i: (i, 0)),
            pl.BlockSpec((1, D_MODEL), lambda i: (0, 0)),
            _mod_spec(sc_idx, tm, rows_per_mod, mod_off),
            _mod_spec(sh_idx, tm, rows_per_mod, mod_off),
            tab, tab,
        ],
        out_specs=[out, out],
        out_shape=[jax.ShapeDtypeStruct((m, D_MODEL), BF16)] * 2,
        compiler_params=_params("parallel"),
        name="norm_chan_dft",
    )(x, g.reshape(1, D_MODEL), mod4, mod4, cc, sn)


def _pos_dft_kernel(cn_ref, sn_ref, p_ref, q_ref, o_ref):
    o_ref[...] = (_dot(cn_ref[...], p_ref[...]) - _dot(sn_ref[...], q_ref[...])).astype(o_ref.dtype)


def _pos_dft(p3, q3, tn):
    bsz, n, _ = p3.shape
    cn, sn = _dft_tables(n)
    cn = jnp.asarray(cn).astype(BF16)
    sn = jnp.asarray(sn).astype(BF16)
    tab = pl.BlockSpec((n, n), lambda b, j: (0, 0))
    blk = pl.BlockSpec((None, n, tn), lambda b, j: (b, 0, j))
    return pl.pallas_call(
        _pos_dft_kernel,
        grid=(bsz, D_MODEL // tn),
        in_specs=[tab, tab, blk, blk],
        out_specs=blk,
        out_shape=jax.ShapeDtypeStruct((bsz, n, D_MODEL), BF16),
        compiler_params=_params("parallel", "parallel"),
        name="pos_dft",
    )(cn, sn, p3, q3)


FFN_SCHED = {name: i for i, name in enumerate((
    "up_rows", "up_mod", "up_col", "conv_col", "down_rows", "down_mod", "down_col",
    "norm_due", "reset_due", "out_due"))}


def _ffn_kernel(sched_ref, xc_ref, g_ref, sc_ref, sh_ref, xp_ref, gate_ref, wg_ref, wv_ref, cp_ref, wd_ref,
                mask_ref, fin_ref, o_ref, h_scr, acc_scr, u0_scr, u1_scr, act0_scr, act1_scr, *,
                final_norm, pieces):
    t = pl.program_id(0)
    tm, tf = act0_scr.shape
    lanes = mask_ref.shape[2]

    def due(name):
        return sched_ref[FFN_SCHED[name], t] != 0

    @pl.when(t == 0)
    def _():
        for ref in (u0_scr, u1_scr, act0_scr, act1_scr, acc_scr):
            ref[...] = jnp.zeros_like(ref)

    @pl.when(due("norm_due"))
    def _():
        h = _rms(xc_ref[...], g_ref[...]) * (1.0 + sc_ref[...]) + sh_ref[...]
        h_scr[...] = h.astype(BF16)

    @pl.when(due("reset_due"))
    def _():
        acc_scr[...] = jnp.zeros_like(acc_scr)

    cb_rows = FFN_CONV_ROWS
    n_row_blocks = tm // cb_rows
    mask_rows = mask_ref.shape[1]

    def conv(u_old, k, r, cs):
        p0 = 4 * k
        ext = u_old[k, r * cb_rows:(r + 1) * cb_rows + 2 * HALO, cs]
        n_ext = cb_rows + 2 * HALO
        ms = slice((r * cb_rows) % mask_rows, (r * cb_rows) % mask_rows + cb_rows)
        inner = slice(HALO, HALO + cb_rows)
        prev = pltpu.roll(ext, 1, axis=0)[inner] * mask_ref[0, ms, :]
        nxt = pltpu.roll(ext, n_ext - 1, axis=0)[inner] * mask_ref[1, ms, :]
        return (prev * cp_ref[p0:p0 + 1, cs] + ext[inner] * cp_ref[p0 + 1:p0 + 2, cs]
                + nxt * cp_ref[p0 + 2:p0 + 3, cs] + cp_ref[p0 + 3:p0 + 4, cs])

    def conv_block(u_old, act_new, idx):
        c, r = divmod(idx, n_row_blocks)
        cs = slice(c * lanes, (c + 1) * lanes)
        gate = conv(u_old, 0, r, cs)
        val = conv(u_old, 1, r, cs)
        act_new[r * cb_rows:(r + 1) * cb_rows, cs] = (_silu(gate) * val).astype(BF16)

    def stages(u_new, u_old, act_new, act_old):
        n_blocks = (tf // lanes) * n_row_blocks
        dn = D_MODEL // pieces
        un = 2 * tf // pieces
        per_half = tf // un
        h = h_scr[...]
        act = act_old[...]
        work = [1] * pieces + [2] * pieces
        early = n_blocks - FFN_TAIL_BLOCKS
        total = sum(work[:-1])
        done = 0
        for piece in range(2 * pieces):
            if piece < pieces:
                ns = slice(piece * dn, (piece + 1) * dn)
                acc_scr[:, ns] += _dot(act, wd_ref[:, ns])
            else:
                k, q = divmod(piece - pieces, per_half)
                ns = slice(q * un, (q + 1) * un)
                u_new[k, HALO:HALO + tm, ns] = _dot(h, (wg_ref, wv_ref)[k][:, ns])
            if piece == 2 * pieces - 1:
                target = n_blocks
            else:
                target = min(early, -(-early * sum(work[:piece + 1]) // total))
            for idx in range(done, target):
                conv_block(u_old, act_new, idx)
            done = target

    @pl.when(t % 2 == 0)
    def _():
        stages(u0_scr, u1_scr, act1_scr, act0_scr)

    @pl.when(t % 2 == 1)
    def _():
        stages(u1_scr, u0_scr, act0_scr, act1_scr)

    @pl.when(due("out_due"))
    def _():
        y = xp_ref[...] + gate_ref[...] * acc_scr[...]
        if final_norm:
            y = _rms(y, fin_ref[...])
        o_ref[...] = y


def _ffn_conv_params(cw, cb, tf):
    cp = jnp.concatenate([cw[:, :D_FF], cb[None, :D_FF], cw[:, D_FF:], cb[None, D_FF:]], axis=0)
    return cp.reshape(8, D_FF // tf, tf).transpose(1, 0, 2)


def _conv_ffn(x, g, mod4, w_up, conv_p, w_down, fin, rows_per_mod, mod_off, row_len, tm,
              final_norm, pieces):
    m = x.shape[0]
    nf, _, tf = conv_p.shape
    n_tiles = (m // tm) * nf
    rb = max(row_len, FFN_CONV_ROWS)
    assert rb % row_len == 0 and rb % FFN_CONV_ROWS == 0 and tm % rb == 0
    pos = np.arange(rb) % row_len
    mask = np.stack([np.broadcast_to((pos != 0)[:, None], (rb, 128)),
                     np.broadcast_to((pos != row_len - 1)[:, None], (rb, 128))]).astype(np.float32)

    t = np.arange(n_tiles + 2)
    up = np.minimum(t, n_tiles - 1)
    cv = np.clip(t - 1, 0, n_tiles - 1)
    dn = np.clip(t - 2, 0, n_tiles - 1)
    sched = np.zeros((len(FFN_SCHED), n_tiles + 2), np.int32)
    sched[FFN_SCHED["up_rows"]] = up // nf
    sched[FFN_SCHED["up_mod"]] = mod_off + (up // nf * tm) // rows_per_mod
    sched[FFN_SCHED["up_col"]] = up % nf
    sched[FFN_SCHED["conv_col"]] = cv % nf
    sched[FFN_SCHED["down_rows"]] = dn // nf
    sched[FFN_SCHED["down_mod"]] = mod_off + (dn // nf * tm) // rows_per_mod
    sched[FFN_SCHED["down_col"]] = dn % nf
    sched[FFN_SCHED["norm_due"]] = (t < n_tiles) & (t % nf == 0)
    sched[FFN_SCHED["reset_due"]] = (t >= 2) & ((t - 2) % nf == 0)
    sched[FFN_SCHED["out_due"]] = (t >= 2) & ((t - 2) % nf == nf - 1)

    def row(name):
        return FFN_SCHED[name]

    def mod_spec(which, name):
        return pl.BlockSpec((None, None, 1, D_MODEL), lambda t, s: (s[row(name), t], which, 0, 0))

    vec = pl.BlockSpec((1, D_MODEL), lambda t, s: (0, 0))
    grid_spec = pltpu.PrefetchScalarGridSpec(
        num_scalar_prefetch=1,
        grid=(n_tiles + 2,),
        in_specs=[
            pl.BlockSpec((tm, D_MODEL), lambda t, s: (s[row("up_rows"), t], 0)),
            vec,
            mod_spec(4, "up_mod"),
            mod_spec(3, "up_mod"),
            pl.BlockSpec((tm, D_MODEL), lambda t, s: (s[row("down_rows"), t], 0)),
            mod_spec(5, "down_mod"),
            pl.BlockSpec((D_MODEL, tf), lambda t, s: (0, s[row("up_col"), t])),
            pl.BlockSpec((D_MODEL, tf), lambda t, s: (0, nf + s[row("up_col"), t])),
            pl.BlockSpec((None, 8, tf), lambda t, s: (s[row("conv_col"), t], 0, 0)),
            pl.BlockSpec((tf, D_MODEL), lambda t, s: (s[row("down_col"), t], 0)),
            pl.BlockSpec((2, rb, 128), lambda t, s: (0, 0, 0)),
            vec,
        ],
        out_specs=pl.BlockSpec((tm, D_MODEL), lambda t, s: (s[row("down_rows"), t], 0)),
        scratch_shapes=[pltpu.VMEM((tm, D_MODEL), BF16), pltpu.VMEM((tm, D_MODEL), F32),
                        pltpu.VMEM((2, tm + 2 * HALO, tf), F32),
                        pltpu.VMEM((2, tm + 2 * HALO, tf), F32),
                        pltpu.VMEM((tm, tf), BF16), pltpu.VMEM((tm, tf), BF16)],
    )
    return pl.pallas_call(
        functools.partial(_ffn_kernel, final_norm=final_norm, pieces=pieces),
        grid_spec=grid_spec,
        out_shape=jax.ShapeDtypeStruct((m, D_MODEL), F32),
        compiler_params=_params("arbitrary"),
        name="conv_ffn",
    )(jnp.asarray(sched), x, g.reshape(1, D_MODEL), mod4, mod4, x, mod4, w_up, w_up, conv_p, w_down,
      jnp.asarray(mask), fin.reshape(1, D_MODEL))


def kernel(x_prompt, x_sample, state_l0_hgrn, c, c_ctx, mod_w_0, mod_b_0, norm1_0, w_in_0, hgrn_lb, hgrn_gnorm_0, gmlp_vnorm_0, gmlp_ws_0, gmlp_bs_0, w_out_0, norm2_0, ffn_up_0, ffn_conv_w_0, ffn_conv_b_0, ffn_down_0, mod_w_1, mod_b_1, norm1_1, w_out_1, norm2_1, ffn_up_1, ffn_conv_w_1, ffn_conv_b_1, ffn_down_1, final_norm):
    n_dec = c.shape[0]
    cond = jnp.concatenate(
        [c, c_ctx[None, :], jnp.zeros((MOD_ROWS - n_dec - 1, D_MODEL), F32)], axis=0)
    mod4 = [_modulation(cond, w, b).reshape(MOD_ROWS, 6, 1, D_MODEL)
            for w, b in ((mod_w_0, mod_b_0), (mod_w_1, mod_b_1))]

    w_in = w_in_0.astype(BF16)
    w_out0 = w_out_0.astype(BF16)
    w_out1 = w_out_1.astype(BF16)
    ffn = [(norm2_0, ffn_up_0.astype(BF16), _ffn_conv_params(ffn_conv_w_0, ffn_conv_b_0, FFN_TF),
            ffn_down_0.astype(BF16)),
           (norm2_1, ffn_up_1.astype(BF16), _ffn_conv_params(ffn_conv_w_1, ffn_conv_b_1, FFN_TF),
            ffn_down_1.astype(BF16))]

    def trunk(x3, s0, mod_off, per_batch_mod, row_len, emit_state):
        bsz, n, _ = x3.shape
        m = bsz * n
        rows_per_mod = n if per_batch_mod else m
        tm = ROW_TILE
        x = x3.reshape(m, D_MODEL)

        proj = _norm_proj(x, norm1_0, mod4[0], 1, 0, w_in, rows_per_mod, mod_off, PROJ_ROW_TILE,
                          PROJ_COL_TILE, F32)
        res = _gla(proj.reshape(bsz, n, IN_WIDTH), hgrn_lb, hgrn_gnorm_0, s0,
                   emit_state)
        out_b = _gmlp(proj, gmlp_vnorm_0, gmlp_ws_0, gmlp_bs_0, ROW_TILE)
        x = _resid_proj(x, mod4[0], 2, [res[0].reshape(m, A_WIDTH), out_b], w_out0,
                        rows_per_mod, mod_off, tm)
        x = _conv_ffn(x, ffn[0][0], mod4[0], *ffn[0][1:], final_norm, rows_per_mod, mod_off,
                      row_len, tm, False, FFN_MXU_PIECES[0])

        p, q = _norm_chan_dft(x, norm1_1, mod4[1], 1, 0, rows_per_mod, mod_off, tm)
        four = _pos_dft(p.reshape(bsz, n, D_MODEL), q.reshape(bsz, n, D_MODEL), POS_DFT_COL_TILE)
        x = _resid_proj(x, mod4[1], 2, [four.reshape(m, D_MODEL)], w_out1, rows_per_mod, mod_off, tm)
        x = _conv_ffn(x, ffn[1][0], mod4[1], *ffn[1][1:], final_norm, rows_per_mod, mod_off,
                      row_len, tm, True, FFN_MXU_PIECES[1])
        return x.reshape(bsz, n, D_MODEL), (res[1] if emit_state else None)

    zero_state = jnp.zeros((x_prompt.shape[0], 2, A_HEADS, A_DK, A_DV), F32)
    y_prompt, state_new = trunk(x_prompt, zero_state, n_dec, False, x_prompt.shape[1], True)
    y_sample, _ = trunk(x_sample, state_l0_hgrn, 0, True, GRID_W, False)
    return (y_prompt, y_sample, state_new.astype(x_prompt.dtype))
```

```python
import functools

import numpy as np
import jax
import jax.numpy as jnp
from jax import lax
from jax.experimental import pallas as pl
from jax.experimental.pallas import tpu as pltpu

D_MODEL = 2048
A_WIDTH = 1024
A_HEADS = 8
A_DK = 128
A_DV = 128
B_WIDTH = 1024
B_GROUPS = 4
B_CHUNK = 128
B_CG = B_WIDTH // B_GROUPS
C_GROUPS = 4
C_CG = D_MODEL // C_GROUPS
SCAN_CHUNK = 64
D_FF = 5632
GRID_W = 64
IN_WIDTH = 5 * A_WIDTH + 2 * B_WIDTH
EPS = 1e-6

F32 = jnp.float32
BF16 = jnp.bfloat16

VMEM_LIMIT = 56 * 1024 * 1024
MOD_ROWS = 16
GLA_ROW_BLOCK = 256
FFN_CONV_ROWS = 64
FFN_MXU_PIECES = (4, 8)
FFN_TAIL_BLOCKS = 1
HALO = 8
NORM_ROW_BLOCK = 16
NORM_UNROLL = 8
ROW_TILE = 512
FFN_TF = 512
PROJ_ROW_TILE = 1024
PROJ_COL_TILE = 1024
POS_DFT_COL_TILE = 512


def _params(*sem):
    return pltpu.CompilerParams(dimension_semantics=sem, vmem_limit_bytes=VMEM_LIMIT)


def _sigmoid(x):
    return 1.0 / (1.0 + jnp.exp(-x))


def _silu(x):
    return x * _sigmoid(x)


def _gelu_tanh(x):
    c = np.float32(np.sqrt(2.0 / np.pi))
    return 0.5 * x * (1.0 + jnp.tanh(c * (x + 0.044715 * (x * x * x))))


def _rms(x, g):
    return x * lax.rsqrt(jnp.mean(x * x, axis=-1, keepdims=True) + EPS) * g


def _norm_mod_rows(x_ref, g_ref, sc_ref, sh_ref, h_ref):
    rb = NORM_ROW_BLOCK

    def body(i, carry):
        rows = pl.ds(pl.multiple_of(i * rb, rb), rb)
        x = x_ref[rows, :]
        y = x * lax.rsqrt(jnp.mean(x * x, axis=-1, keepdims=True) + EPS)
        h_ref[rows, :] = (y * (g_ref[...] * (1.0 + sc_ref[...])) + sh_ref[...]).astype(h_ref.dtype)
        return carry

    lax.fori_loop(0, x_ref.shape[0] // rb, body, 0, unroll=NORM_UNROLL)


def _dot(a, b):
    return jnp.dot(a, b, preferred_element_type=F32)


def _dot_nt(a, b):
    return lax.dot_general(a, b, (((1,), (1,)), ((), ())), preferred_element_type=F32)


def _mod_kernel(c_ref, w_ref, b_ref, o_ref):
    s = _silu(c_ref[...]).astype(BF16)
    o_ref[...] = _dot(s, w_ref[...].astype(BF16)) + b_ref[...]


def _modulation(cond, w, b):
    tn = 1024
    n = w.shape[1]
    return pl.pallas_call(
        _mod_kernel,
        grid=(n // tn,),
        in_specs=[
            pl.BlockSpec((MOD_ROWS, D_MODEL), lambda j: (0, 0)),
            pl.BlockSpec((D_MODEL, tn), lambda j: (0, j)),
            pl.BlockSpec((1, tn), lambda j: (0, j)),
        ],
        out_specs=pl.BlockSpec((MOD_ROWS, tn), lambda j: (0, j)),
        out_shape=jax.ShapeDtypeStruct((MOD_ROWS, n), F32),
        compiler_params=_params("arbitrary"),
        name="modulation",
    )(cond, w, b.reshape(1, n))


def _mod_spec(which, tm, rows_per_mod, mod_off):
    return pl.BlockSpec(
        (None, None, 1, D_MODEL),
        lambda i, *_: (mod_off + (i * tm) // rows_per_mod, which, 0, 0))


def _norm_proj_kernel(x_ref, g_ref, sc_ref, sh_ref, w_ref, o_ref, h_scr):
    @pl.when(pl.program_id(1) == 0)
    def _():
        _norm_mod_rows(x_ref, g_ref, sc_ref, sh_ref, h_scr)

    o_ref[...] = _dot(h_scr[...], w_ref[...]).astype(o_ref.dtype)


def _norm_proj(x, g, mod4, sc_idx, sh_idx, w, rows_per_mod, mod_off, tm, tn, out_dtype):
    m = x.shape[0]
    n = w.shape[1]
    return pl.pallas_call(
        _norm_proj_kernel,
        grid=(m // tm, n // tn),
        in_specs=[
            pl.BlockSpec((tm, D_MODEL), lambda i, j: (i, 0)),
            pl.BlockSpec((1, D_MODEL), lambda i, j: (0, 0)),
            _mod_spec(sc_idx, tm, rows_per_mod, mod_off),
            _mod_spec(sh_idx, tm, rows_per_mod, mod_off),
            pl.BlockSpec((D_MODEL, tn), lambda i, j: (0, j)),
        ],
        out_specs=pl.BlockSpec((tm, tn), lambda i, j: (i, j)),
        out_shape=jax.ShapeDtypeStruct((m, n), out_dtype),
        scratch_shapes=[pltpu.VMEM((tm, D_MODEL), BF16)],
        compiler_params=_params("parallel", "arbitrary"),
        name="norm_proj",
    )(x, g.reshape(1, D_MODEL), mod4, mod4, w)


def _gla_kernel(qa_ref, fzf_ref, fzb_ref, ia_ref, ga_ref, lbp_ref, gn_ref, s0_ref,
                out_ref, *rest, n, emit_state):
    if emit_state:
        snew_ref, ops_scr, dec_scr, p_scr, kv_scr, o_scr, st_scr = rest
    else:
        ops_scr, dec_scr, p_scr, kv_scr, o_scr, st_scr = rest
    c_len = SCAN_CHUNK
    nc = n // c_len
    mid = c_len // 2
    rb = GLA_ROW_BLOCK
    cpb = rb // c_len

    def lower_bound(d):
        a = [lbp_ref[d, l] for l in range(lbp_ref.shape[1])]
        mx = functools.reduce(jnp.maximum, a)
        e = [jnp.exp(t - mx) for t in a]
        return e[0] / functools.reduce(lambda u, w: u + w, e)

    lbs = (lower_bound(0), lower_bound(1))
    pos = lax.broadcasted_iota(jnp.int32, (rb, A_DK), 0) % c_len

    def operands(i, carry):
        r0 = pl.multiple_of(i * rb, rb)
        rows = pl.ds(r0, rb)
        q = _silu(qa_ref[rows, :]) * (A_DK ** -0.5)
        for d, fz_ref in enumerate((fzf_ref, fzb_ref)):
            lb = lbs[d]
            f = lb + (1.0 - lb) * _sigmoid(fz_ref[rows, :])
            k = 1.0 - f
            b = jnp.log2(f)
            for s in (1, 2, 4, 8, 16, 32):
                if d == 0:
                    b = b + jnp.where(pos >= s, pltpu.roll(b, s, axis=0), 0.0)
                else:
                    b = b + jnp.where(pos < c_len - s, pltpu.roll(b, rb - s, axis=0), 0.0)
            for ci in range(cpb):
                sl = slice(ci * c_len, (ci + 1) * c_len)
                bc = b[sl]
                if d == 0:
                    ref, b_last = bc[mid - 1:mid], bc[c_len - 1:c_len]
                else:
                    ref, b_last = bc[c_len - mid:c_len - mid + 1], bc[0:1]
                qe = q[sl] * jnp.exp2(bc - ref)
                ke = k[sl] * jnp.exp2(ref - bc)
                crow = pl.ds(r0 + ci * c_len, c_len)
                ops_scr[d, 0, crow, :] = qe.astype(BF16)
                ops_scr[d, 1, crow, :] = ke.astype(BF16)
                ops_scr[d, 2, crow, :] = (qe * jnp.exp2(ref)).astype(BF16)
                ops_scr[d, 3, crow, :] = (ke * jnp.exp2(b_last - ref)).astype(BF16)
                dec_scr[d, i * cpb + ci] = jnp.broadcast_to(jnp.exp2(b_last), (8, A_DK))
        return carry

    lax.fori_loop(0, n // rb, operands, 0)

    ri = lax.broadcasted_iota(jnp.int32, (c_len, c_len), 0)
    ci_ = lax.broadcasted_iota(jnp.int32, (c_len, c_len), 1)
    keep = (ri >= ci_, ri <= ci_)
    unroll = True if nc <= 4 else 8

    def chunk_products(c, carry):
        rows = pl.ds(pl.multiple_of(c * c_len, c_len), c_len)
        vt = ia_ref[rows, :].T.astype(BF16)
        for d in range(2):
            scores = jnp.where(keep[d], _dot_nt(ops_scr[d, 0, rows, :], ops_scr[d, 1, rows, :]), 0.0)
            p_scr[d, rows, :] = scores.astype(BF16)
            kv_scr[d, c] = _dot(vt, ops_scr[d, 3, rows, :])
        return carry

    lax.fori_loop(0, nc, chunk_products, 0, unroll=unroll)

    for d in range(2):
        st_scr[d] = s0_ref[d].T

    def scan_step(c, carry):
        for d in range(2):
            cidx = c if d == 0 else nc - 1 - c
            rows = pl.ds(pl.multiple_of(cidx * c_len, c_len), c_len)
            st = st_scr[d]
            o_scr[d, rows, :] = (_dot(p_scr[d, rows, :], ia_ref[rows, :].astype(BF16))
                                 + _dot_nt(ops_scr[d, 2, rows, :], st.astype(BF16)))
            st_scr[d] = dec_scr[d, cidx][0:1, :] * st + kv_scr[d, cidx]
        return carry

    lax.fori_loop(0, nc, scan_step, 0, unroll=unroll)
    if emit_state:
        for d in range(2):
            snew_ref[d] = st_scr[d].T

    def finish(i, carry):
        rows = pl.ds(pl.multiple_of(i * rb, rb), rb)
        o = _rms(o_scr[0, rows, :] + o_scr[1, rows, :], gn_ref[...]) * _silu(ga_ref[rows, :])
        out_ref[rows, :] = o.astype(out_ref.dtype)
        return carry

    lax.fori_loop(0, n // rb, finish, 0)


def _gla(proj3, hgrn_lb, gnorm, s0, emit_state):
    bsz, n, _ = proj3.shape
    h = A_HEADS

    def col(k):
        return pl.BlockSpec((None, n, A_DK), lambda b, hh: (b, 0, k * h + hh))

    n_lb = hgrn_lb.shape[1]
    lb_spec = pl.BlockSpec((2, n_lb, None, 1, A_DK), lambda b, hh: (0, 0, hh, 0, 0))
    st_spec = pl.BlockSpec((None, 2, None, A_DK, A_DV), lambda b, hh: (b, 0, hh, 0, 0))
    out_shape = [jax.ShapeDtypeStruct((bsz, n, A_WIDTH), BF16)]
    out_specs = [pl.BlockSpec((None, n, A_DV), lambda b, hh: (b, 0, hh))]
    if emit_state:
        out_shape.append(jax.ShapeDtypeStruct((bsz, 2, h, A_DK, A_DV), F32))
        out_specs.append(st_spec)
    res = pl.pallas_call(
        functools.partial(_gla_kernel, n=n, emit_state=emit_state),
        grid=(bsz, h),
        in_specs=[col(0), col(1), col(2), col(3), col(4), lb_spec,
                  pl.BlockSpec((1, A_DV), lambda b, hh: (0, 0)), st_spec],
        out_specs=out_specs,
        out_shape=out_shape,
        scratch_shapes=[pltpu.VMEM((2, 4, n, A_DK), BF16),
                        pltpu.VMEM((2, n // SCAN_CHUNK, 8, A_DK), F32),
                        pltpu.VMEM((2, n, SCAN_CHUNK), BF16),
                        pltpu.VMEM((2, n // SCAN_CHUNK, A_DV, A_DK), F32),
                        pltpu.VMEM((2, n, A_DV), F32),
                        pltpu.VMEM((2, A_DV, A_DK), F32)],
        compiler_params=_params("parallel", "parallel"),
        name="hgrn2",
    )(proj3, proj3, proj3, proj3, proj3, hgrn_lb.reshape(2, n_lb, h, 1, A_DK),
      gnorm.reshape(1, A_DV), s0)
    return res


def _gmlp_kernel(ub_ref, vb_ref, vn_ref, ws_ref, bst_ref, out_ref, *, rows):
    for g in range(B_GROUPS):
        cs = slice(g * B_CG, (g + 1) * B_CG)
        vv = _rms(_gelu_tanh(vb_ref[:, cs]), vn_ref[:, cs]).astype(BF16)
        w = ws_ref[g].astype(BF16)
        bias = bst_ref[:, g:g + 1]
        for c in range(rows // B_CHUNK):
            rs = slice(c * B_CHUNK, (c + 1) * B_CHUNK)
            mixed = _dot(w, vv[rs, :]) + bias
            out_ref[rs, cs] = (_gelu_tanh(ub_ref[rs, cs]) * mixed).astype(out_ref.dtype)


def _gmlp(proj, vnorm, ws, bs, rows):
    m = proj.shape[0]
    ub_blk = 5 * A_WIDTH // B_WIDTH
    return pl.pallas_call(
        functools.partial(_gmlp_kernel, rows=rows),
        grid=(m // rows,),
        in_specs=[
            pl.BlockSpec((rows, B_WIDTH), lambda i: (i, ub_blk)),
            pl.BlockSpec((rows, B_WIDTH), lambda i: (i, ub_blk + 1)),
            pl.BlockSpec((1, B_WIDTH), lambda i: (0, 0)),
            pl.BlockSpec((B_GROUPS, B_CHUNK, B_CHUNK), lambda i: (0, 0, 0)),
            pl.BlockSpec((B_CHUNK, B_GROUPS), lambda i: (0, 0)),
        ],
        out_specs=pl.BlockSpec((rows, B_WIDTH), lambda i: (i, 0)),
        out_shape=jax.ShapeDtypeStruct((m, B_WIDTH), BF16),
        compiler_params=_params("parallel"),
        name="gmlp",
    )(proj, proj, vnorm.reshape(1, B_WIDTH), ws, bs.T)


def _resid_proj_kernel(*refs, n_parts):
    x_ref, gate_ref = refs[0], refs[1]
    a_refs = refs[2:2 + n_parts]
    w_refs = refs[2 + n_parts:2 + 2 * n_parts]
    o_ref = refs[2 + 2 * n_parts]
    acc = _dot(a_refs[0][...], w_refs[0][...])
    for a_ref, w_ref in zip(a_refs[1:], w_refs[1:]):
        acc = acc + _dot(a_ref[...], w_ref[...])
    o_ref[...] = x_ref[...] + gate_ref[...] * acc


def _resid_proj(x, mod4, gate_idx, parts, w, rows_per_mod, mod_off, tm):
    m = x.shape[0]
    n_parts = len(parts)
    kp = parts[0].shape[1]
    in_specs = [pl.BlockSpec((tm, D_MODEL), lambda i: (i, 0)),
                _mod_spec(gate_idx, tm, rows_per_mod, mod_off)]
    in_specs += [pl.BlockSpec((tm, kp), lambda i: (i, 0)) for _ in parts]
    in_specs += [pl.BlockSpec((kp, D_MODEL), lambda i, p=p: (p, 0)) for p in range(n_parts)]
    return pl.pallas_call(
        functools.partial(_resid_proj_kernel, n_parts=n_parts),
        grid=(m // tm,),
        in_specs=in_specs,
        out_specs=pl.BlockSpec((tm, D_MODEL), lambda i: (i, 0)),
        out_shape=jax.ShapeDtypeStruct((m, D_MODEL), F32),
        compiler_params=_params("parallel"),
        name="resid_proj",
    )(x, mod4, *parts, *([w] * n_parts))


def _dft_tables(n):
    idx = np.arange(n, dtype=np.int64)
    ang = 2.0 * np.pi * ((idx[:, None] * idx[None, :]) % n).astype(np.float64) / n
    s = 1.0 / np.sqrt(n)
    return (np.cos(ang) * s).astype(np.float32), (np.sin(ang) * s).astype(np.float32)


def _norm_chan_dft_kernel(x_ref, g_ref, sc_ref, sh_ref, cc_ref, sn_ref, p_ref, q_ref, h_scr):
    _norm_mod_rows(x_ref, g_ref, sc_ref, sh_ref, h_scr)
    for g in range(C_GROUPS):
        cs = slice(g * C_CG, (g + 1) * C_CG)
        p_ref[:, cs] = _dot(h_scr[:, cs], cc_ref[...]).astype(p_ref.dtype)
        q_ref[:, cs] = _dot(h_scr[:, cs], sn_ref[...]).astype(q_ref.dtype)


def _norm_chan_dft(x, g, mod4, sc_idx, sh_idx, rows_per_mod, mod_off, tm):
    m = x.shape[0]
    cc, sn = _dft_tables(C_CG)
    cc = jnp.asarray(cc).astype(BF16)
    sn = jnp.asarray(sn).astype(BF16)
    tab = pl.BlockSpec((C_CG, C_CG), lambda i: (0, 0))
    out = pl.BlockSpec((tm, D_MODEL), lambda i: (i, 0))
    return pl.pallas_call(
        _norm_chan_dft_kernel,
        grid=(m // tm,),
        in_specs=[
            pl.BlockSpec((tm, D_MODEL), lambda i: (i, 0)),
            pl.BlockSpec((1, D_MODEL), lambda i: (0, 0)),
            _mod_spec(sc_idx, tm, rows_per_mod, mod_off),
            _mod_spec(sh_idx, tm, rows_per_mod, mod_off),
            tab, tab,
        ],
        out_specs=[out, out],
        out_shape=[jax.ShapeDtypeStruct((m, D_MODEL), BF16)] * 2,
        scratch_shapes=[pltpu.VMEM((tm, D_MODEL), BF16)],
        compiler_params=_params("parallel"),
        name="norm_chan_dft",
    )(x, g.reshape(1, D_MODEL), mod4, mod4, cc, sn)


def _pos_dft_kernel(cn_ref, sn_ref, p_ref, q_ref, o_ref):
    o_ref[...] = (_dot(cn_ref[...], p_ref[...]) - _dot(sn_ref[...], q_ref[...])).astype(o_ref.dtype)


def _pos_dft(p3, q3, tn):
    bsz, n, _ = p3.shape
    cn, sn = _dft_tables(n)
    cn = jnp.asarray(cn).astype(BF16)
    sn = jnp.asarray(sn).astype(BF16)
    tab = pl.BlockSpec((n, n), lambda b, j: (0, 0))
    blk = pl.BlockSpec((None, n, tn), lambda b, j: (b, 0, j))
    return pl.pallas_call(
        _pos_dft_kernel,
        grid=(bsz, D_MODEL // tn),
        in_specs=[tab, tab, blk, blk],
        out_specs=blk,
        out_shape=jax.ShapeDtypeStruct((bsz, n, D_MODEL), BF16),
        compiler_params=_params("parallel", "parallel"),
        name="pos_dft",
    )(cn, sn, p3, q3)


FFN_SCHED = {name: i for i, name in enumerate((
    "up_rows", "up_mod", "up_col", "conv_col", "down_rows", "down_mod", "down_col",
    "norm_due", "reset_due", "out_due"))}


def _ffn_kernel(sched_ref, xc_ref, g_ref, sc_ref, sh_ref, xp_ref, gate_ref, wg_ref, wv_ref, cp_ref, wd_ref,
                mask_ref, fin_ref, o_ref, h_scr, acc_scr, u0_scr, u1_scr, act0_scr, act1_scr, *,
                final_norm, pieces):
    t = pl.program_id(0)
    tm, tf = act0_scr.shape
    lanes = mask_ref.shape[2]

    def due(name):
        return sched_ref[FFN_SCHED[name], t] != 0

    @pl.when(t == 0)
    def _():
        for ref in (u0_scr, u1_scr, act0_scr, act1_scr, acc_scr):
            ref[...] = jnp.zeros_like(ref)

    @pl.when(due("norm_due"))
    def _():
        _norm_mod_rows(xc_ref, g_ref, sc_ref, sh_ref, h_scr)

    @pl.when(due("reset_due"))
    def _():
        acc_scr[...] = jnp.zeros_like(acc_scr)

    cb_rows = FFN_CONV_ROWS
    n_row_blocks = tm // cb_rows
    mask_rows = mask_ref.shape[1]

    def conv(u_old, k, r, cs):
        p0 = 4 * k
        ext = u_old[k, r * cb_rows:(r + 1) * cb_rows + 2 * HALO, cs]
        n_ext = cb_rows + 2 * HALO
        ms = slice((r * cb_rows) % mask_rows, (r * cb_rows) % mask_rows + cb_rows)
        inner = slice(HALO, HALO + cb_rows)
        prev = pltpu.roll(ext, 1, axis=0)[inner] * mask_ref[0, ms, :]
        nxt = pltpu.roll(ext, n_ext - 1, axis=0)[inner] * mask_ref[1, ms, :]
        return (prev * cp_ref[p0:p0 + 1, cs] + ext[inner] * cp_ref[p0 + 1:p0 + 2, cs]
                + nxt * cp_ref[p0 + 2:p0 + 3, cs] + cp_ref[p0 + 3:p0 + 4, cs])

    def conv_block(u_old, act_new, idx):
        c, r = divmod(idx, n_row_blocks)
        cs = slice(c * lanes, (c + 1) * lanes)
        gate = conv(u_old, 0, r, cs)
        val = conv(u_old, 1, r, cs)
        act_new[r * cb_rows:(r + 1) * cb_rows, cs] = (_silu(gate) * val).astype(BF16)

    def stages(u_new, u_old, act_new, act_old):
        n_blocks = (tf // lanes) * n_row_blocks
        up_pieces, down_pieces = pieces
        un = 2 * tf // up_pieces
        dn = D_MODEL // down_pieces
        per_half = tf // un
        h = h_scr[...]
        act = act_old[...]
        mxu = ([("up", q, un * D_MODEL) for q in range(up_pieces)]
               + [("down", q, dn * tf) for q in range(down_pieces)])
        early = n_blocks - FFN_TAIL_BLOCKS
        total = sum(w for _, _, w in mxu[:-1])
        done = 0
        spent = 0
        for n, (kind, q, w) in enumerate(mxu):
            if kind == "down":
                ns = slice(q * dn, (q + 1) * dn)
                acc_scr[:, ns] += _dot(act, wd_ref[:, ns])
            else:
                k, qq = divmod(q, per_half)
                ns = slice(qq * un, (qq + 1) * un)
                u_new[k, HALO:HALO + tm, ns] = _dot(h, (wg_ref, wv_ref)[k][:, ns])
            spent += w
            target = n_blocks if n == len(mxu) - 1 else min(early, -(-early * spent // total))
            for idx in range(done, target):
                conv_block(u_old, act_new, idx)
            done = target

    @pl.when(t % 2 == 0)
    def _():
        stages(u0_scr, u1_scr, act1_scr, act0_scr)

    @pl.when(t % 2 == 1)
    def _():
        stages(u1_scr, u0_scr, act0_scr, act1_scr)

    @pl.when(due("out_due"))
    def _():
        rb = NORM_ROW_BLOCK

        def body(i, carry):
            rows = pl.ds(pl.multiple_of(i * rb, rb), rb)
            y = xp_ref[rows, :] + gate_ref[...] * acc_scr[rows, :]
            if final_norm:
                y = _rms(y, fin_ref[...])
            o_ref[rows, :] = y
            acc_scr[rows, :] = jnp.zeros((rb, D_MODEL), F32)
            return carry

        lax.fori_loop(0, tm // rb, body, 0, unroll=NORM_UNROLL)


def _ffn_conv_params(cw, cb, tf):
    cp = jnp.concatenate([cw[:, :D_FF], cb[None, :D_FF], cw[:, D_FF:], cb[None, D_FF:]], axis=0)
    return cp.reshape(8, D_FF // tf, tf).transpose(1, 0, 2)


def _conv_ffn(x, g, mod4, w_up, conv_p, w_down, fin, rows_per_mod, mod_off, row_len, tm,
              final_norm, pieces):
    m = x.shape[0]
    nf, _, tf = conv_p.shape
    n_tiles = (m // tm) * nf
    rb = max(row_len, FFN_CONV_ROWS)
    assert rb % row_len == 0 and rb % FFN_CONV_ROWS == 0 and tm % rb == 0
    pos = np.arange(rb) % row_len
    mask = np.stack([np.broadcast_to((pos != 0)[:, None], (rb, 128)),
                     np.broadcast_to((pos != row_len - 1)[:, None], (rb, 128))]).astype(np.float32)

    t = np.arange(n_tiles + 2)
    up = np.minimum(t, n_tiles - 1)
    cv = np.clip(t - 1, 0, n_tiles - 1)
    dn = np.clip(t - 2, 0, n_tiles - 1)
    sched = np.zeros((len(FFN_SCHED), n_tiles + 2), np.int32)
    sched[FFN_SCHED["up_rows"]] = up // nf
    sched[FFN_SCHED["up_mod"]] = mod_off + (up // nf * tm) // rows_per_mod
    sched[FFN_SCHED["up_col"]] = up % nf
    sched[FFN_SCHED["conv_col"]] = cv % nf
    sched[FFN_SCHED["down_rows"]] = dn // nf
    sched[FFN_SCHED["down_mod"]] = mod_off + (dn // nf * tm) // rows_per_mod
    sched[FFN_SCHED["down_col"]] = dn % nf
    sched[FFN_SCHED["norm_due"]] = (t < n_tiles) & (t % nf == 0)
    sched[FFN_SCHED["reset_due"]] = t == 2
    sched[FFN_SCHED["out_due"]] = (t >= 2) & ((t - 2) % nf == nf - 1)

    def row(name):
        return FFN_SCHED[name]

    def mod_spec(which, name):
        return pl.BlockSpec((None, None, 1, D_MODEL), lambda t, s: (s[row(name), t], which, 0, 0))

    vec = pl.BlockSpec((1, D_MODEL), lambda t, s: (0, 0))
    grid_spec = pltpu.PrefetchScalarGridSpec(
        num_scalar_prefetch=1,
        grid=(n_tiles + 2,),
        in_specs=[
            pl.BlockSpec((tm, D_MODEL), lambda t, s: (s[row("up_rows"), t], 0)),
            vec,
            mod_spec(4, "up_mod"),
            mod_spec(3, "up_mod"),
            pl.BlockSpec((tm, D_MODEL), lambda t, s: (s[row("down_rows"), t], 0)),
            mod_spec(5, "down_mod"),
            pl.BlockSpec((D_MODEL, tf), lambda t, s: (0, s[row("up_col"), t])),
            pl.BlockSpec((D_MODEL, tf), lambda t, s: (0, nf + s[row("up_col"), t])),
            pl.BlockSpec((None, 8, tf), lambda t, s: (s[row("conv_col"), t], 0, 0)),
            pl.BlockSpec((tf, D_MODEL), lambda t, s: (s[row("down_col"), t], 0)),
            pl.BlockSpec((2, rb, 128), lambda t, s: (0, 0, 0)),
            vec,
        ],
        out_specs=pl.BlockSpec((tm, D_MODEL), lambda t, s: (s[row("down_rows"), t], 0)),
        scratch_shapes=[pltpu.VMEM((tm, D_MODEL), BF16), pltpu.VMEM((tm, D_MODEL), F32),
                        pltpu.VMEM((2, tm + 2 * HALO, tf), F32),
                        pltpu.VMEM((2, tm + 2 * HALO, tf), F32),
                        pltpu.VMEM((tm, tf), BF16), pltpu.VMEM((tm, tf), BF16)],
    )
    return pl.pallas_call(
        functools.partial(_ffn_kernel, final_norm=final_norm, pieces=pieces),
        grid_spec=grid_spec,
        out_shape=jax.ShapeDtypeStruct((m, D_MODEL), F32),
        compiler_params=_params("arbitrary"),
        name="conv_ffn",
    )(jnp.asarray(sched), x, g.reshape(1, D_MODEL), mod4, mod4, x, mod4, w_up, w_up, conv_p, w_down,
      jnp.asarray(mask), fin.reshape(1, D_MODEL))


def kernel(x_prompt, x_sample, state_l0_hgrn, c, c_ctx, mod_w_0, mod_b_0, norm1_0, w_in_0, hgrn_lb, hgrn_gnorm_0, gmlp_vnorm_0, gmlp_ws_0, gmlp_bs_0, w_out_0, norm2_0, ffn_up_0, ffn_conv_w_0, ffn_conv_b_0, ffn_down_0, mod_w_1, mod_b_1, norm1_1, w_out_1, norm2_1, ffn_up_1, ffn_conv_w_1, ffn_conv_b_1, ffn_down_1, final_norm):
    n_dec = c.shape[0]
    cond = jnp.concatenate(
        [c, c_ctx[None, :], jnp.zeros((MOD_ROWS - n_dec - 1, D_MODEL), F32)], axis=0)
    mod4 = [_modulation(cond, w, b).reshape(MOD_ROWS, 6, 1, D_MODEL)
            for w, b in ((mod_w_0, mod_b_0), (mod_w_1, mod_b_1))]

    w_in = w_in_0.astype(BF16)
    w_out0 = w_out_0.astype(BF16)
    w_out1 = w_out_1.astype(BF16)
    ffn = [(norm2_0, ffn_up_0.astype(BF16), _ffn_conv_params(ffn_conv_w_0, ffn_conv_b_0, FFN_TF),
            ffn_down_0.astype(BF16)),
           (norm2_1, ffn_up_1.astype(BF16), _ffn_conv_params(ffn_conv_w_1, ffn_conv_b_1, FFN_TF),
            ffn_down_1.astype(BF16))]

    def trunk(x3, s0, mod_off, per_batch_mod, row_len, emit_state):
        bsz, n, _ = x3.shape
        m = bsz * n
        rows_per_mod = n if per_batch_mod else m
        tm = ROW_TILE
        x = x3.reshape(m, D_MODEL)

        proj = _norm_proj(x, norm1_0, mod4[0], 1, 0, w_in, rows_per_mod, mod_off, PROJ_ROW_TILE,
                          PROJ_COL_TILE, F32)
        res = _gla(proj.reshape(bsz, n, IN_WIDTH), hgrn_lb, hgrn_gnorm_0, s0,
                   emit_state)
        out_b = _gmlp(proj, gmlp_vnorm_0, gmlp_ws_0, gmlp_bs_0, ROW_TILE)
        x = _resid_proj(x, mod4[0], 2, [res[0].reshape(m, A_WIDTH), out_b], w_out0,
                        rows_per_mod, mod_off, tm)
        x = _conv_ffn(x, ffn[0][0], mod4[0], *ffn[0][1:], final_norm, rows_per_mod, mod_off,
                      row_len, tm, False, FFN_MXU_PIECES)

        p, q = _norm_chan_dft(x, norm1_1, mod4[1], 1, 0, rows_per_mod, mod_off, tm)
        four = _pos_dft(p.reshape(bsz, n, D_MODEL), q.reshape(bsz, n, D_MODEL), POS_DFT_COL_TILE)
        x = _resid_proj(x, mod4[1], 2, [four.reshape(m, D_MODEL)], w_out1, rows_per_mod, mod_off, tm)
        x = _conv_ffn(x, ffn[1][0], mod4[1], *ffn[1][1:], final_norm, rows_per_mod, mod_off,
                      row_len, tm, True, FFN_MXU_PIECES)
        return x.reshape(bsz, n, D_MODEL), (res[1] if emit_state else None)

    zero_state = jnp.zeros((x_prompt.shape[0], 2, A_HEADS, A_DK, A_DV), F32)
    y_prompt, state_new = trunk(x_prompt, zero_state, n_dec, False, x_prompt.shape[1], True)
    y_sample, _ = trunk(x_sample, state_l0_hgrn, 0, True, GRID_W, False)
    return (y_prompt, y_sample, state_new.astype(x_prompt.dtype))
```

```python
import functools

import numpy as np
import jax
import jax.numpy as jnp
from jax import lax
from jax.experimental import pallas as pl
from jax.experimental.pallas import tpu as pltpu

D_MODEL = 2048
A_WIDTH = 1024
A_HEADS = 8
A_DK = 128
A_DV = 128
B_WIDTH = 1024
B_GROUPS = 4
B_CHUNK = 128
B_CG = B_WIDTH // B_GROUPS
C_GROUPS = 4
C_CG = D_MODEL // C_GROUPS
SCAN_CHUNK = 64
D_FF = 5632
GRID_W = 64
IN_WIDTH = 5 * A_WIDTH + 2 * B_WIDTH
EPS = 1e-6

F32 = jnp.float32
BF16 = jnp.bfloat16

VMEM_LIMIT = 56 * 1024 * 1024
MOD_ROWS = 16
GLA_ROW_BLOCK = 256
GLA_HEADS_PER_STEP = 2
FFN_CONV_ROWS = 64
FFN_MXU_PIECES = (4, 8)
FFN_TAIL_BLOCKS = 1
HALO = 8
NORM_ROW_BLOCK = 16
NORM_UNROLL = 8
ROW_TILE = 512
FFN_TF = 512
PROJ_ROW_TILE = 1024
PROJ_COL_TILE = 1792
POS_DFT_COL_TILE = 1024
CAST_ROWS = 512
GMLP_ROWS = 1024


def _params(*sem):
    return pltpu.CompilerParams(dimension_semantics=sem, vmem_limit_bytes=VMEM_LIMIT)


def _sigmoid(x):
    return 1.0 / (1.0 + jnp.exp(-x))


def _silu(x):
    return x * _sigmoid(x)


def _gelu_tanh(x):
    c = np.float32(np.sqrt(2.0 / np.pi))
    return 0.5 * x * (1.0 + jnp.tanh(c * (x + 0.044715 * (x * x * x))))


def _rms(x, g):
    return x * lax.rsqrt(jnp.mean(x * x, axis=-1, keepdims=True) + EPS) * g


def _norm_mod_rows(x_ref, g_ref, sc_ref, sh_ref, h_ref):
    rb = NORM_ROW_BLOCK

    def body(i, carry):
        rows = pl.ds(pl.multiple_of(i * rb, rb), rb)
        x = x_ref[rows, :]
        y = x * lax.rsqrt(jnp.mean(x * x, axis=-1, keepdims=True) + EPS)
        h_ref[rows, :] = (y * (g_ref[...] * (1.0 + sc_ref[...])) + sh_ref[...]).astype(h_ref.dtype)
        return carry

    lax.fori_loop(0, x_ref.shape[0] // rb, body, 0, unroll=NORM_UNROLL)


def _dot(a, b):
    return jnp.dot(a, b, preferred_element_type=F32)


def _dot_nt(a, b):
    return lax.dot_general(a, b, (((1,), (1,)), ((), ())), preferred_element_type=F32)


def _cast_kernel(w_ref, o_ref):
    o_ref[...] = w_ref[...].astype(o_ref.dtype)


def _cast_col_tiles(w, tn):
    k, n = w.shape
    return pl.pallas_call(
        _cast_kernel,
        grid=(n // tn,),
        in_specs=[pl.BlockSpec((k, tn), lambda j: (0, j))],
        out_specs=pl.BlockSpec((None, k, tn), lambda j: (j, 0, 0)),
        out_shape=jax.ShapeDtypeStruct((n // tn, k, tn), BF16),
        compiler_params=_params("parallel"),
        name="cast_col_tiles",
    )(w)


def _cast_rows(w, tk):
    k, n = w.shape
    return pl.pallas_call(
        _cast_kernel,
        grid=(k // tk,),
        in_specs=[pl.BlockSpec((tk, n), lambda i: (i, 0))],
        out_specs=pl.BlockSpec((tk, n), lambda i: (i, 0)),
        out_shape=jax.ShapeDtypeStruct((k, n), BF16),
        compiler_params=_params("parallel"),
        name="cast_rows",
    )(w)


def _mod_kernel(c_ref, w_ref, b_ref, o_ref):
    s = _silu(c_ref[...]).astype(BF16)
    o_ref[...] = _dot(s, w_ref[...].astype(BF16)) + b_ref[...]


def _modulation(cond, w, b):
    tn = 1024
    n = w.shape[1]
    return pl.pallas_call(
        _mod_kernel,
        grid=(n // tn,),
        in_specs=[
            pl.BlockSpec((MOD_ROWS, D_MODEL), lambda j: (0, 0)),
            pl.BlockSpec((D_MODEL, tn), lambda j: (0, j)),
            pl.BlockSpec((1, tn), lambda j: (0, j)),
        ],
        out_specs=pl.BlockSpec((MOD_ROWS, tn), lambda j: (0, j)),
        out_shape=jax.ShapeDtypeStruct((MOD_ROWS, n), F32),
        compiler_params=_params("arbitrary"),
        name="modulation",
    )(cond, w, b.reshape(1, n))


def _mod_spec(which, tm, rows_per_mod, mod_off):
    return pl.BlockSpec(
        (None, None, 1, D_MODEL),
        lambda i, *_: (mod_off + (i * tm) // rows_per_mod, which, 0, 0))


def _norm_proj_kernel(x_ref, g_ref, sc_ref, sh_ref, w_ref, o_ref, h_scr):
    @pl.when(pl.program_id(1) == 0)
    def _():
        _norm_mod_rows(x_ref, g_ref, sc_ref, sh_ref, h_scr)

    o_ref[...] = _dot(h_scr[...], w_ref[...]).astype(o_ref.dtype)


def _norm_proj(x, g, mod4, sc_idx, sh_idx, w_tiles, rows_per_mod, mod_off, tm, out_dtype):
    m = x.shape[0]
    nt, _, tn = w_tiles.shape
    n = nt * tn
    return pl.pallas_call(
        _norm_proj_kernel,
        grid=(m // tm, n // tn),
        in_specs=[
            pl.BlockSpec((tm, D_MODEL), lambda i, j: (i, 0)),
            pl.BlockSpec((1, D_MODEL), lambda i, j: (0, 0)),
            _mod_spec(sc_idx, tm, rows_per_mod, mod_off),
            _mod_spec(sh_idx, tm, rows_per_mod, mod_off),
            pl.BlockSpec((None, D_MODEL, tn), lambda i, j: (j, 0, 0)),
        ],
        out_specs=pl.BlockSpec((tm, tn), lambda i, j: (i, j)),
        out_shape=jax.ShapeDtypeStruct((m, n), out_dtype),
        scratch_shapes=[pltpu.VMEM((tm, D_MODEL), BF16)],
        compiler_params=_params("parallel", "arbitrary"),
        name="norm_proj",
    )(x, g.reshape(1, D_MODEL), mod4, mod4, w_tiles)


def _gla_head(qa_ref, fzf_ref, fzb_ref, ia_ref, ga_ref, lbp_ref, gn_ref, s0_ref,
              out_ref, *rest, n, emit_state):
    if emit_state:
        snew_ref, ops_scr, dec_scr, p_scr, kv_scr, o_scr, st_scr = rest
    else:
        ops_scr, dec_scr, p_scr, kv_scr, o_scr, st_scr = rest
    c_len = SCAN_CHUNK
    nc = n // c_len
    mid = c_len // 2
    rb = GLA_ROW_BLOCK
    cpb = rb // c_len

    def lower_bound(d):
        a = [lbp_ref[d, l] for l in range(lbp_ref.shape[1])]
        mx = functools.reduce(jnp.maximum, a)
        e = [jnp.exp(t - mx) for t in a]
        return e[0] / functools.reduce(lambda u, w: u + w, e)

    lbs = (lower_bound(0), lower_bound(1))
    pos = lax.broadcasted_iota(jnp.int32, (rb, A_DK), 0) % c_len

    def operands(i, carry):
        r0 = pl.multiple_of(i * rb, rb)
        rows = pl.ds(r0, rb)
        q = _silu(qa_ref[rows, :]) * (A_DK ** -0.5)
        for d, fz_ref in enumerate((fzf_ref, fzb_ref)):
            lb = lbs[d]
            f = lb + (1.0 - lb) * _sigmoid(fz_ref[rows, :])
            k = 1.0 - f
            b = jnp.log2(f)
            for s in (1, 2, 4, 8, 16, 32):
                if d == 0:
                    b = b + jnp.where(pos >= s, pltpu.roll(b, s, axis=0), 0.0)
                else:
                    b = b + jnp.where(pos < c_len - s, pltpu.roll(b, rb - s, axis=0), 0.0)
            for ci in range(cpb):
                sl = slice(ci * c_len, (ci + 1) * c_len)
                bc = b[sl]
                if d == 0:
                    ref, b_last = bc[mid - 1:mid], bc[c_len - 1:c_len]
                else:
                    ref, b_last = bc[c_len - mid:c_len - mid + 1], bc[0:1]
                qe = q[sl] * jnp.exp2(bc - ref)
                ke = k[sl] * jnp.exp2(ref - bc)
                crow = pl.ds(r0 + ci * c_len, c_len)
                ops_scr[d, 0, crow, :] = qe.astype(BF16)
                ops_scr[d, 1, crow, :] = ke.astype(BF16)
                ops_scr[d, 2, crow, :] = (qe * jnp.exp2(ref)).astype(BF16)
                ops_scr[d, 3, crow, :] = (ke * jnp.exp2(b_last - ref)).astype(BF16)
                dec_scr[d, i * cpb + ci] = jnp.broadcast_to(jnp.exp2(b_last), (8, A_DK))
        return carry

    lax.fori_loop(0, n // rb, operands, 0)

    ri = lax.broadcasted_iota(jnp.int32, (c_len, c_len), 0)
    ci_ = lax.broadcasted_iota(jnp.int32, (c_len, c_len), 1)
    keep = (ri >= ci_, ri <= ci_)
    unroll = True if nc <= 4 else 8

    def chunk_products(c, carry):
        rows = pl.ds(pl.multiple_of(c * c_len, c_len), c_len)
        vt = ia_ref[rows, :].T.astype(BF16)
        for d in range(2):
            scores = jnp.where(keep[d], _dot_nt(ops_scr[d, 0, rows, :], ops_scr[d, 1, rows, :]), 0.0)
            p_scr[d, rows, :] = scores.astype(BF16)
            kv_scr[d, c] = _dot(vt, ops_scr[d, 3, rows, :])
        return carry

    lax.fori_loop(0, nc, chunk_products, 0, unroll=unroll)

    for d in range(2):
        st_scr[d] = s0_ref[d].T

    def scan_step(c, carry):
        for d in range(2):
            cidx = c if d == 0 else nc - 1 - c
            rows = pl.ds(pl.multiple_of(cidx * c_len, c_len), c_len)
            st = st_scr[d]
            o_scr[d, rows, :] = (_dot(p_scr[d, rows, :], ia_ref[rows, :].astype(BF16))
                                 + _dot_nt(ops_scr[d, 2, rows, :], st.astype(BF16)))
            st_scr[d] = dec_scr[d, cidx][0:1, :] * st + kv_scr[d, cidx]
        return carry

    lax.fori_loop(0, nc, scan_step, 0, unroll=unroll)
    if emit_state:
        for d in range(2):
            snew_ref[d] = st_scr[d].T

    def finish(i, carry):
        rows = pl.ds(pl.multiple_of(i * rb, rb), rb)
        o = _rms(o_scr[0, rows, :] + o_scr[1, rows, :], gn_ref[...]) * _silu(ga_ref[rows, :])
        out_ref[rows, :] = o.astype(out_ref.dtype)
        return carry

    lax.fori_loop(0, n // rb, finish, 0)


def _gla_kernel(qa_ref, fzf_ref, fzb_ref, ia_ref, ga_ref, lbp_ref, gn_ref, s0_ref, out_ref, *rest,
                n, emit_state, heads):
    for hh in range(heads):
        cols = pl.ds(hh * A_DK, A_DK)
        head_rest = (rest[0].at[:, hh],) + tuple(rest[1:]) if emit_state else rest
        _gla_head(qa_ref.at[:, cols], fzf_ref.at[:, cols], fzb_ref.at[:, cols], ia_ref.at[:, cols],
                  ga_ref.at[:, cols], lbp_ref.at[:, :, hh], gn_ref, s0_ref.at[:, hh],
                  out_ref.at[:, cols], *head_rest, n=n, emit_state=emit_state)


def _gla(proj3, hgrn_lb, gnorm, s0, emit_state):
    bsz, n, _ = proj3.shape
    h = A_HEADS
    g = GLA_HEADS_PER_STEP
    hg = h // g

    def col(k):
        return pl.BlockSpec((None, n, g * A_DK), lambda b, hh: (b, 0, k * hg + hh))

    n_lb = hgrn_lb.shape[1]
    lb_spec = pl.BlockSpec((2, n_lb, g, 1, A_DK), lambda b, hh: (0, 0, hh, 0, 0))
    st_spec = pl.BlockSpec((None, 2, g, A_DK, A_DV), lambda b, hh: (b, 0, hh, 0, 0))
    out_shape = [jax.ShapeDtypeStruct((bsz, n, A_WIDTH), BF16)]
    out_specs = [pl.BlockSpec((None, n, g * A_DV), lambda b, hh: (b, 0, hh))]
    if emit_state:
        out_shape.append(jax.ShapeDtypeStruct((bsz, 2, h, A_DK, A_DV), F32))
        out_specs.append(st_spec)
    res = pl.pallas_call(
        functools.partial(_gla_kernel, n=n, emit_state=emit_state, heads=g),
        grid=(bsz, hg),
        in_specs=[col(0), col(1), col(2), col(3), col(4), lb_spec,
                  pl.BlockSpec((1, A_DV), lambda b, hh: (0, 0)), st_spec],
        out_specs=out_specs,
        out_shape=out_shape,
        scratch_shapes=[pltpu.VMEM((2, 4, n, A_DK), BF16),
                        pltpu.VMEM((2, n // SCAN_CHUNK, 8, A_DK), F32),
                        pltpu.VMEM((2, n, SCAN_CHUNK), BF16),
                        pltpu.VMEM((2, n // SCAN_CHUNK, A_DV, A_DK), F32),
                        pltpu.VMEM((2, n, A_DV), F32),
                        pltpu.VMEM((2, A_DV, A_DK), F32)],
        compiler_params=_params("parallel", "parallel"),
        name="hgrn2",
    )(proj3, proj3, proj3, proj3, proj3, hgrn_lb.reshape(2, n_lb, h, 1, A_DK),
      gnorm.reshape(1, A_DV), s0)
    return res


def _gmlp_kernel(ub_ref, vb_ref, vn_ref, ws_ref, bst_ref, out_ref, *, rows):
    for g in range(B_GROUPS):
        cs = slice(g * B_CG, (g + 1) * B_CG)
        vv = _rms(_gelu_tanh(vb_ref[:, cs]), vn_ref[:, cs]).astype(BF16)
        w = ws_ref[g].astype(BF16)
        bias = bst_ref[:, g:g + 1]
        for c in range(rows // B_CHUNK):
            rs = slice(c * B_CHUNK, (c + 1) * B_CHUNK)
            mixed = _dot(w, vv[rs, :]) + bias
            out_ref[rs, cs] = (_gelu_tanh(ub_ref[rs, cs]) * mixed).astype(out_ref.dtype)


def _gmlp(proj, vnorm, ws, bs, rows):
    m = proj.shape[0]
    ub_blk = 5 * A_WIDTH // B_WIDTH
    return pl.pallas_call(
        functools.partial(_gmlp_kernel, rows=rows),
        grid=(m // rows,),
        in_specs=[
            pl.BlockSpec((rows, B_WIDTH), lambda i: (i, ub_blk)),
            pl.BlockSpec((rows, B_WIDTH), lambda i: (i, ub_blk + 1)),
            pl.BlockSpec((1, B_WIDTH), lambda i: (0, 0)),
            pl.BlockSpec((B_GROUPS, B_CHUNK, B_CHUNK), lambda i: (0, 0, 0)),
            pl.BlockSpec((B_CHUNK, B_GROUPS), lambda i: (0, 0)),
        ],
        out_specs=pl.BlockSpec((rows, B_WIDTH), lambda i: (i, 0)),
        out_shape=jax.ShapeDtypeStruct((m, B_WIDTH), BF16),
        compiler_params=_params("parallel"),
        name="gmlp",
    )(proj, proj, vnorm.reshape(1, B_WIDTH), ws, bs.T)


def _resid_proj_kernel(*refs, n_parts):
    x_ref, gate_ref = refs[0], refs[1]
    a_refs = refs[2:2 + n_parts]
    w_refs = refs[2 + n_parts:2 + 2 * n_parts]
    o_ref = refs[2 + 2 * n_parts]
    acc = _dot(a_refs[0][...], w_refs[0][...])
    for a_ref, w_ref in zip(a_refs[1:], w_refs[1:]):
        acc = acc + _dot(a_ref[...], w_ref[...])
    o_ref[...] = x_ref[...] + gate_ref[...] * acc


def _resid_proj(x, mod4, gate_idx, parts, w, rows_per_mod, mod_off, tm):
    m = x.shape[0]
    n_parts = len(parts)
    kp = parts[0].shape[1]
    in_specs = [pl.BlockSpec((tm, D_MODEL), lambda i: (i, 0)),
                _mod_spec(gate_idx, tm, rows_per_mod, mod_off)]
    in_specs += [pl.BlockSpec((tm, kp), lambda i: (i, 0)) for _ in parts]
    in_specs += [pl.BlockSpec((kp, D_MODEL), lambda i, p=p: (p, 0)) for p in range(n_parts)]
    return pl.pallas_call(
        functools.partial(_resid_proj_kernel, n_parts=n_parts),
        grid=(m // tm,),
        in_specs=in_specs,
        out_specs=pl.BlockSpec((tm, D_MODEL), lambda i: (i, 0)),
        out_shape=jax.ShapeDtypeStruct((m, D_MODEL), F32),
        compiler_params=_params("parallel"),
        name="resid_proj",
    )(x, mod4, *parts, *([w] * n_parts))


def _dft_tables(n):
    idx = np.arange(n, dtype=np.int64)
    ang = 2.0 * np.pi * ((idx[:, None] * idx[None, :]) % n).astype(np.float64) / n
    s = 1.0 / np.sqrt(n)
    return (np.cos(ang) * s).astype(np.float32), (np.sin(ang) * s).astype(np.float32)


def _norm_chan_dft_kernel(x_ref, g_ref, sc_ref, sh_ref, cc_ref, sn_ref, p_ref, q_ref, h_scr):
    _norm_mod_rows(x_ref, g_ref, sc_ref, sh_ref, h_scr)
    for g in range(C_GROUPS):
        cs = slice(g * C_CG, (g + 1) * C_CG)
        p_ref[:, cs] = _dot(h_scr[:, cs], cc_ref[...]).astype(p_ref.dtype)
        q_ref[:, cs] = _dot(h_scr[:, cs], sn_ref[...]).astype(q_ref.dtype)


def _norm_chan_dft(x, g, mod4, sc_idx, sh_idx, rows_per_mod, mod_off, tm):
    m = x.shape[0]
    cc, sn = _dft_tables(C_CG)
    cc = jnp.asarray(cc).astype(BF16)
    sn = jnp.asarray(sn).astype(BF16)
    tab = pl.BlockSpec((C_CG, C_CG), lambda i: (0, 0))
    out = pl.BlockSpec((tm, D_MODEL), lambda i: (i, 0))
    return pl.pallas_call(
        _norm_chan_dft_kernel,
        grid=(m // tm,),
        in_specs=[
            pl.BlockSpec((tm, D_MODEL), lambda i: (i, 0)),
            pl.BlockSpec((1, D_MODEL), lambda i: (0, 0)),
            _mod_spec(sc_idx, tm, rows_per_mod, mod_off),
            _mod_spec(sh_idx, tm, rows_per_mod, mod_off),
            tab, tab,
        ],
        out_specs=[out, out],
        out_shape=[jax.ShapeDtypeStruct((m, D_MODEL), BF16)] * 2,
        scratch_shapes=[pltpu.VMEM((tm, D_MODEL), BF16)],
        compiler_params=_params("parallel"),
        name="norm_chan_dft",
    )(x, g.reshape(1, D_MODEL), mod4, mod4, cc, sn)


def _pos_dft_kernel(cn_ref, sn_ref, p_ref, q_ref, o_ref):
    o_ref[...] = (_dot(cn_ref[...], p_ref[...]) - _dot(sn_ref[...], q_ref[...])).astype(o_ref.dtype)


def _pos_dft(p3, q3, tn):
    bsz, n, _ = p3.shape
    cn, sn = _dft_tables(n)
    cn = jnp.asarray(cn).astype(BF16)
    sn = jnp.asarray(sn).astype(BF16)
    tab = pl.BlockSpec((n, n), lambda b, j: (0, 0))
    blk = pl.BlockSpec((None, n, tn), lambda b, j: (b, 0, j))
    return pl.pallas_call(
        _pos_dft_kernel,
        grid=(bsz, D_MODEL // tn),
        in_specs=[tab, tab, blk, blk],
        out_specs=blk,
        out_shape=jax.ShapeDtypeStruct((bsz, n, D_MODEL), BF16),
        compiler_params=_params("parallel", "parallel"),
        name="pos_dft",
    )(cn, sn, p3, q3)


FFN_SCHED = {name: i for i, name in enumerate((
    "up_rows", "up_mod", "up_col", "conv_col", "down_rows", "down_mod", "down_col",
    "norm_due", "reset_due", "out_due"))}


def _ffn_kernel(sched_ref, xc_ref, g_ref, sc_ref, sh_ref, xp_ref, gate_ref, wg_ref, wv_ref, cp_ref, wd_ref,
                mask_ref, fin_ref, o_ref, h_scr, acc_scr, u0_scr, u1_scr, act0_scr, act1_scr, *,
                final_norm, pieces):
    t = pl.program_id(0)
    tm, tf = act0_scr.shape
    lanes = mask_ref.shape[2]

    def due(name):
        return sched_ref[FFN_SCHED[name], t] != 0

    @pl.when(t == 0)
    def _():
        for ref in (u0_scr, u1_scr, act0_scr, act1_scr, acc_scr):
            ref[...] = jnp.zeros_like(ref)

    @pl.when(due("norm_due"))
    def _():
        _norm_mod_rows(xc_ref, g_ref, sc_ref, sh_ref, h_scr)

    @pl.when(due("reset_due"))
    def _():
        acc_scr[...] = jnp.zeros_like(acc_scr)

    cb_rows = FFN_CONV_ROWS
    n_row_blocks = tm // cb_rows
    mask_rows = mask_ref.shape[1]

    def conv(u_old, k, r, cs):
        p0 = 4 * k
        ext = u_old[k, r * cb_rows:(r + 1) * cb_rows + 2 * HALO, cs]
        n_ext = cb_rows + 2 * HALO
        ms = slice((r * cb_rows) % mask_rows, (r * cb_rows) % mask_rows + cb_rows)
        inner = slice(HALO, HALO + cb_rows)
        prev = pltpu.roll(ext, 1, axis=0)[inner] * mask_ref[0, ms, :]
        nxt = pltpu.roll(ext, n_ext - 1, axis=0)[inner] * mask_ref[1, ms, :]
        return (prev * cp_ref[p0:p0 + 1, cs] + ext[inner] * cp_ref[p0 + 1:p0 + 2, cs]
                + nxt * cp_ref[p0 + 2:p0 + 3, cs] + cp_ref[p0 + 3:p0 + 4, cs])

    def conv_block(u_old, act_new, idx):
        c, r = divmod(idx, n_row_blocks)
        cs = slice(c * lanes, (c + 1) * lanes)
        gate = conv(u_old, 0, r, cs)
        val = conv(u_old, 1, r, cs)
        act_new[r * cb_rows:(r + 1) * cb_rows, cs] = (_silu(gate) * val).astype(BF16)

    def stages(u_new, u_old, act_new, act_old):
        n_blocks = (tf // lanes) * n_row_blocks
        up_pieces, down_pieces = pieces
        un = 2 * tf // up_pieces
        dn = D_MODEL // down_pieces
        per_half = tf // un
        h = h_scr[...]
        act = act_old[...]
        mxu = ([("up", q, un * D_MODEL) for q in range(up_pieces)]
               + [("down", q, dn * tf) for q in range(down_pieces)])
        early = n_blocks - FFN_TAIL_BLOCKS
        total = sum(w for _, _, w in mxu[:-1])
        done = 0
        spent = 0
        for n, (kind, q, w) in enumerate(mxu):
            if kind == "down":
                ns = slice(q * dn, (q + 1) * dn)
                acc_scr[:, ns] += _dot(act, wd_ref[:, ns])
            else:
                k, qq = divmod(q, per_half)
                ns = slice(qq * un, (qq + 1) * un)
                u_new[k, HALO:HALO + tm, ns] = _dot(h, (wg_ref, wv_ref)[k][:, ns])
            spent += w
            target = n_blocks if n == len(mxu) - 1 else min(early, -(-early * spent // total))
            for idx in range(done, target):
                conv_block(u_old, act_new, idx)
            done = target

    @pl.when(t % 2 == 0)
    def _():
        stages(u0_scr, u1_scr, act1_scr, act0_scr)

    @pl.when(t % 2 == 1)
    def _():
        stages(u1_scr, u0_scr, act0_scr, act1_scr)

    @pl.when(due("out_due"))
    def _():
        rb = NORM_ROW_BLOCK

        def body(i, carry):
            rows = pl.ds(pl.multiple_of(i * rb, rb), rb)
            y = xp_ref[rows, :] + gate_ref[...] * acc_scr[rows, :]
            if final_norm:
                y = _rms(y, fin_ref[...])
            o_ref[rows, :] = y
            acc_scr[rows, :] = jnp.zeros((rb, D_MODEL), F32)
            return carry

        lax.fori_loop(0, tm // rb, body, 0, unroll=NORM_UNROLL)


def _ffn_conv_params(cw, cb, tf):
    cp = jnp.concatenate([cw[:, :D_FF], cb[None, :D_FF], cw[:, D_FF:], cb[None, D_FF:]], axis=0)
    return cp.reshape(8, D_FF // tf, tf).transpose(1, 0, 2)


def _conv_ffn(x, g, mod4, w_up, conv_p, w_down, fin, rows_per_mod, mod_off, row_len, tm,
              final_norm, pieces):
    m = x.shape[0]
    nf, _, tf = conv_p.shape
    n_tiles = (m // tm) * nf
    rb = max(row_len, FFN_CONV_ROWS)
    assert rb % row_len == 0 and rb % FFN_CONV_ROWS == 0 and tm % rb == 0
    pos = np.arange(rb) % row_len
    mask = np.stack([np.broadcast_to((pos != 0)[:, None], (rb, 128)),
                     np.broadcast_to((pos != row_len - 1)[:, None], (rb, 128))]).astype(np.float32)

    t = np.arange(n_tiles + 2)
    up = np.minimum(t, n_tiles - 1)
    cv = np.clip(t - 1, 0, n_tiles - 1)
    dn = np.clip(t - 2, 0, n_tiles - 1)
    sched = np.zeros((len(FFN_SCHED), n_tiles + 2), np.int32)
    sched[FFN_SCHED["up_rows"]] = up // nf
    sched[FFN_SCHED["up_mod"]] = mod_off + (up // nf * tm) // rows_per_mod
    sched[FFN_SCHED["up_col"]] = up % nf
    sched[FFN_SCHED["conv_col"]] = cv % nf
    sched[FFN_SCHED["down_rows"]] = dn // nf
    sched[FFN_SCHED["down_mod"]] = mod_off + (dn // nf * tm) // rows_per_mod
    sched[FFN_SCHED["down_col"]] = dn % nf
    sched[FFN_SCHED["norm_due"]] = (t < n_tiles) & (t % nf == 0)
    sched[FFN_SCHED["reset_due"]] = t == 2
    sched[FFN_SCHED["out_due"]] = (t >= 2) & ((t - 2) % nf == nf - 1)

    def row(name):
        return FFN_SCHED[name]

    def mod_spec(which, name):
        return pl.BlockSpec((None, None, 1, D_MODEL), lambda t, s: (s[row(name), t], which, 0, 0))

    vec = pl.BlockSpec((1, D_MODEL), lambda t, s: (0, 0))
    grid_spec = pltpu.PrefetchScalarGridSpec(
        num_scalar_prefetch=1,
        grid=(n_tiles + 2,),
        in_specs=[
            pl.BlockSpec((tm, D_MODEL), lambda t, s: (s[row("up_rows"), t], 0)),
            vec,
            mod_spec(4, "up_mod"),
            mod_spec(3, "up_mod"),
            pl.BlockSpec((tm, D_MODEL), lambda t, s: (s[row("down_rows"), t], 0)),
            mod_spec(5, "down_mod"),
            pl.BlockSpec((None, D_MODEL, tf), lambda t, s: (s[row("up_col"), t], 0, 0)),
            pl.BlockSpec((None, D_MODEL, tf), lambda t, s: (nf + s[row("up_col"), t], 0, 0)),
            pl.BlockSpec((None, 8, tf), lambda t, s: (s[row("conv_col"), t], 0, 0)),
            pl.BlockSpec((tf, D_MODEL), lambda t, s: (s[row("down_col"), t], 0)),
            pl.BlockSpec((2, rb, 128), lambda t, s: (0, 0, 0)),
            vec,
        ],
        out_specs=pl.BlockSpec((tm, D_MODEL), lambda t, s: (s[row("down_rows"), t], 0)),
        scratch_shapes=[pltpu.VMEM((tm, D_MODEL), BF16), pltpu.VMEM((tm, D_MODEL), F32),
                        pltpu.VMEM((2, tm + 2 * HALO, tf), F32),
                        pltpu.VMEM((2, tm + 2 * HALO, tf), F32),
                        pltpu.VMEM((tm, tf), BF16), pltpu.VMEM((tm, tf), BF16)],
    )
    return pl.pallas_call(
        functools.partial(_ffn_kernel, final_norm=final_norm, pieces=pieces),
        grid_spec=grid_spec,
        out_shape=jax.ShapeDtypeStruct((m, D_MODEL), F32),
        compiler_params=_params("arbitrary"),
        name="conv_ffn",
    )(jnp.asarray(sched), x, g.reshape(1, D_MODEL), mod4, mod4, x, mod4, w_up, w_up, conv_p, w_down,
      jnp.asarray(mask), fin.reshape(1, D_MODEL))


def kernel(x_prompt, x_sample, state_l0_hgrn, c, c_ctx, mod_w_0, mod_b_0, norm1_0, w_in_0, hgrn_lb, hgrn_gnorm_0, gmlp_vnorm_0, gmlp_ws_0, gmlp_bs_0, w_out_0, norm2_0, ffn_up_0, ffn_conv_w_0, ffn_conv_b_0, ffn_down_0, mod_w_1, mod_b_1, norm1_1, w_out_1, norm2_1, ffn_up_1, ffn_conv_w_1, ffn_conv_b_1, ffn_down_1, final_norm):
    n_dec = c.shape[0]
    cond = jnp.concatenate(
        [c, c_ctx[None, :], jnp.zeros((MOD_ROWS - n_dec - 1, D_MODEL), F32)], axis=0)
    mod4 = [_modulation(cond, w, b).reshape(MOD_ROWS, 6, 1, D_MODEL)
            for w, b in ((mod_w_0, mod_b_0), (mod_w_1, mod_b_1))]

    w_in = _cast_col_tiles(w_in_0, PROJ_COL_TILE)
    w_out0 = _cast_rows(w_out_0, CAST_ROWS)
    w_out1 = _cast_rows(w_out_1, CAST_ROWS)
    ffn = [(norm2_0, _cast_col_tiles(ffn_up_0, FFN_TF), _ffn_conv_params(ffn_conv_w_0, ffn_conv_b_0, FFN_TF),
            _cast_rows(ffn_down_0, CAST_ROWS)),
           (norm2_1, _cast_col_tiles(ffn_up_1, FFN_TF), _ffn_conv_params(ffn_conv_w_1, ffn_conv_b_1, FFN_TF),
            _cast_rows(ffn_down_1, CAST_ROWS))]

    def trunk(x3, s0, mod_off, per_batch_mod, row_len, emit_state):
        bsz, n, _ = x3.shape
        m = bsz * n
        rows_per_mod = n if per_batch_mod else m
        tm = ROW_TILE
        x = x3.reshape(m, D_MODEL)

        proj = _norm_proj(x, norm1_0, mod4[0], 1, 0, w_in, rows_per_mod, mod_off, PROJ_ROW_TILE, F32)
        res = _gla(proj.reshape(bsz, n, IN_WIDTH), hgrn_lb, hgrn_gnorm_0, s0,
                   emit_state)
        out_b = _gmlp(proj, gmlp_vnorm_0, gmlp_ws_0, gmlp_bs_0, GMLP_ROWS)
        x = _resid_proj(x, mod4[0], 2, [res[0].reshape(m, A_WIDTH), out_b], w_out0,
                        rows_per_mod, mod_off, tm)
        x = _conv_ffn(x, ffn[0][0], mod4[0], *ffn[0][1:], final_norm, rows_per_mod, mod_off,
                      row_len, tm, False, FFN_MXU_PIECES)

        p, q = _norm_chan_dft(x, norm1_1, mod4[1], 1, 0, rows_per_mod, mod_off, tm)
        four = _pos_dft(p.reshape(bsz, n, D_MODEL), q.reshape(bsz, n, D_MODEL), POS_DFT_COL_TILE)
        x = _resid_proj(x, mod4[1], 2, [four.reshape(m, D_MODEL)], w_out1, rows_per_mod, mod_off, tm)
        x = _conv_ffn(x, ffn[1][0], mod4[1], *ffn[1][1:], final_norm, rows_per_mod, mod_off,
                      row_len, tm, True, FFN_MXU_PIECES)
        return x.reshape(bsz, n, D_MODEL), (res[1] if emit_state else None)

    zero_state = jnp.zeros((x_prompt.shape[0], 2, A_HEADS, A_DK, A_DV), F32)
    y_prompt, state_new = trunk(x_prompt, zero_state, n_dec, False, x_prompt.shape[1], True)
    y_sample, _ = trunk(x_sample, state_l0_hgrn, 0, True, GRID_W, False)
    return (y_prompt, y_sample, state_new.astype(x_prompt.dtype))
```

```python
import functools

import numpy as np
import jax
import jax.numpy as jnp
from jax import lax
from jax.experimental import pallas as pl
from jax.experimental.pallas import tpu as pltpu

D_MODEL = 2048
A_WIDTH = 1024
A_HEADS = 8
A_DK = 128
A_DV = 128
B_WIDTH = 1024
B_GROUPS = 4
B_CHUNK = 128
B_CG = B_WIDTH // B_GROUPS
C_GROUPS = 4
C_CG = D_MODEL // C_GROUPS
SCAN_CHUNK = 64
D_FF = 5632
GRID_W = 64
IN_WIDTH = 5 * A_WIDTH + 2 * B_WIDTH
EPS = 1e-6

F32 = jnp.float32
BF16 = jnp.bfloat16

VMEM_LIMIT = 56 * 1024 * 1024
MOD_ROWS = 16
GLA_ROW_BLOCK = 256
GLA_HEADS_PER_STEP = 2
GLA_ROWS_PER_STEP = 2048
FFN_CONV_ROWS = 64
FFN_MXU_PIECES = (4, 8)
FFN_TAIL_BLOCKS = 1
HALO = 8
NORM_ROW_BLOCK = 16
NORM_UNROLL = 8
ROW_TILE = 512
FFN_TF = 512
PROJ_ROW_TILE = 1024
PROJ_COL_TILE = 1792
POS_DFT_COL_TILE = 1024
CAST_ROWS = 512
GMLP_ROWS = 1024


def _params(*sem):
    return pltpu.CompilerParams(dimension_semantics=sem, vmem_limit_bytes=VMEM_LIMIT)


def _sigmoid(x):
    return 1.0 / (1.0 + jnp.exp(-x))


def _silu(x):
    return x * _sigmoid(x)


def _gelu_tanh(x):
    c = np.float32(np.sqrt(2.0 / np.pi))
    return 0.5 * x * (1.0 + jnp.tanh(c * (x + 0.044715 * (x * x * x))))


def _rms(x, g):
    return x * lax.rsqrt(jnp.mean(x * x, axis=-1, keepdims=True) + EPS) * g


def _norm_mod_rows(x_ref, g_ref, sc_ref, sh_ref, h_ref):
    rb = NORM_ROW_BLOCK

    def body(i, carry):
        rows = pl.ds(pl.multiple_of(i * rb, rb), rb)
        x = x_ref[rows, :]
        y = x * lax.rsqrt(jnp.mean(x * x, axis=-1, keepdims=True) + EPS)
        h_ref[rows, :] = (y * (g_ref[...] * (1.0 + sc_ref[...])) + sh_ref[...]).astype(h_ref.dtype)
        return carry

    lax.fori_loop(0, x_ref.shape[0] // rb, body, 0, unroll=NORM_UNROLL)


def _dot(a, b):
    return jnp.dot(a, b, preferred_element_type=F32)


def _dot_nt(a, b):
    return lax.dot_general(a, b, (((1,), (1,)), ((), ())), preferred_element_type=F32)


def _cast_kernel(w_ref, o_ref):
    o_ref[...] = w_ref[...].astype(o_ref.dtype)


def _cast_col_tiles(w, tn, pair_halves=False):
    k, n = w.shape
    half = n // tn // 2
    slot = (lambda j: jnp.where(j < half, 2 * j, 2 * (j - half) + 1)) if pair_halves else (lambda j: j)
    return pl.pallas_call(
        _cast_kernel,
        grid=(n // tn,),
        in_specs=[pl.BlockSpec((k, tn), lambda j: (0, j))],
        out_specs=pl.BlockSpec((None, k, tn), lambda j: (slot(j), 0, 0)),
        out_shape=jax.ShapeDtypeStruct((n // tn, k, tn), BF16),
        compiler_params=_params("parallel"),
        name="cast_col_tiles",
    )(w)


def _cast_rows(w, tk):
    k, n = w.shape
    return pl.pallas_call(
        _cast_kernel,
        grid=(k // tk,),
        in_specs=[pl.BlockSpec((tk, n), lambda i: (i, 0))],
        out_specs=pl.BlockSpec((tk, n), lambda i: (i, 0)),
        out_shape=jax.ShapeDtypeStruct((k, n), BF16),
        compiler_params=_params("parallel"),
        name="cast_rows",
    )(w)


def _mod_kernel(c_ref, w_ref, b_ref, o_ref):
    s = _silu(c_ref[...]).astype(BF16)
    o_ref[...] = _dot(s, w_ref[...].astype(BF16)) + b_ref[...]


def _modulation(cond, w, b):
    tn = 1024
    n = w.shape[1]
    return pl.pallas_call(
        _mod_kernel,
        grid=(n // tn,),
        in_specs=[
            pl.BlockSpec((MOD_ROWS, D_MODEL), lambda j: (0, 0)),
            pl.BlockSpec((D_MODEL, tn), lambda j: (0, j)),
            pl.BlockSpec((1, tn), lambda j: (0, j)),
        ],
        out_specs=pl.BlockSpec((MOD_ROWS, tn), lambda j: (0, j)),
        out_shape=jax.ShapeDtypeStruct((MOD_ROWS, n), F32),
        compiler_params=_params("arbitrary"),
        name="modulation",
    )(cond, w, b.reshape(1, n))


def _mod_spec(which, tm, rows_per_mod, mod_off):
    return pl.BlockSpec(
        (None, None, 1, D_MODEL),
        lambda i, *_: (mod_off + (i * tm) // rows_per_mod, which, 0, 0))


def _norm_proj_kernel(x_ref, g_ref, sc_ref, sh_ref, w_ref, o_ref, h_scr):
    @pl.when(pl.program_id(1) == 0)
    def _():
        _norm_mod_rows(x_ref, g_ref, sc_ref, sh_ref, h_scr)

    o_ref[...] = _dot(h_scr[...], w_ref[...]).astype(o_ref.dtype)


def _norm_proj(x, g, mod4, sc_idx, sh_idx, w_tiles, rows_per_mod, mod_off, tm, out_dtype):
    m = x.shape[0]
    nt, _, tn = w_tiles.shape
    n = nt * tn
    return pl.pallas_call(
        _norm_proj_kernel,
        grid=(m // tm, n // tn),
        in_specs=[
            pl.BlockSpec((tm, D_MODEL), lambda i, j: (i, 0)),
            pl.BlockSpec((1, D_MODEL), lambda i, j: (0, 0)),
            _mod_spec(sc_idx, tm, rows_per_mod, mod_off),
            _mod_spec(sh_idx, tm, rows_per_mod, mod_off),
            pl.BlockSpec((None, D_MODEL, tn), lambda i, j: (j, 0, 0)),
        ],
        out_specs=pl.BlockSpec((tm, tn), lambda i, j: (i, j)),
        out_shape=jax.ShapeDtypeStruct((m, n), out_dtype),
        scratch_shapes=[pltpu.VMEM((tm, D_MODEL), BF16)],
        compiler_params=_params("parallel", "arbitrary"),
        name="norm_proj",
    )(x, g.reshape(1, D_MODEL), mod4, mod4, w_tiles)


def _gla_head(qa_ref, fzf_ref, fzb_ref, ia_ref, ga_ref, lbp_ref, gn_ref, s0_ref,
              out_ref, *rest, n, emit_state):
    if emit_state:
        snew_ref, ops_scr, dec_scr, p_scr, kv_scr, o_scr, st_scr = rest
    else:
        ops_scr, dec_scr, p_scr, kv_scr, o_scr, st_scr = rest
    c_len = SCAN_CHUNK
    nc = n // c_len
    mid = c_len // 2
    rb = GLA_ROW_BLOCK
    cpb = rb // c_len

    def lower_bound(d):
        a = [lbp_ref[d, l] for l in range(lbp_ref.shape[1])]
        mx = functools.reduce(jnp.maximum, a)
        e = [jnp.exp(t - mx) for t in a]
        return e[0] / functools.reduce(lambda u, w: u + w, e)

    lbs = (lower_bound(0), lower_bound(1))
    pos = lax.broadcasted_iota(jnp.int32, (rb, A_DK), 0) % c_len

    def operands(i, carry):
        r0 = pl.multiple_of(i * rb, rb)
        rows = pl.ds(r0, rb)
        q = _silu(qa_ref[rows, :]) * (A_DK ** -0.5)
        for d, fz_ref in enumerate((fzf_ref, fzb_ref)):
            lb = lbs[d]
            f = lb + (1.0 - lb) * _sigmoid(fz_ref[rows, :])
            k = 1.0 - f
            b = jnp.log2(f)
            for s in (1, 2, 4, 8, 16, 32):
                if d == 0:
                    b = b + jnp.where(pos >= s, pltpu.roll(b, s, axis=0), 0.0)
                else:
                    b = b + jnp.where(pos < c_len - s, pltpu.roll(b, rb - s, axis=0), 0.0)
            for ci in range(cpb):
                sl = slice(ci * c_len, (ci + 1) * c_len)
                bc = b[sl]
                if d == 0:
                    ref, b_last = bc[mid - 1:mid], bc[c_len - 1:c_len]
                else:
                    ref, b_last = bc[c_len - mid:c_len - mid + 1], bc[0:1]
                qe = q[sl] * jnp.exp2(bc - ref)
                ke = k[sl] * jnp.exp2(ref - bc)
                crow = pl.ds(r0 + ci * c_len, c_len)
                ops_scr[d, 0, crow, :] = qe.astype(BF16)
                ops_scr[d, 1, crow, :] = ke.astype(BF16)
                ops_scr[d, 2, crow, :] = (qe * jnp.exp2(ref)).astype(BF16)
                ops_scr[d, 3, crow, :] = (ke * jnp.exp2(b_last - ref)).astype(BF16)
                dec_scr[d, i * cpb + ci] = jnp.broadcast_to(jnp.exp2(b_last), (8, A_DK))
        return carry

    lax.fori_loop(0, n // rb, operands, 0)

    ri = lax.broadcasted_iota(jnp.int32, (c_len, c_len), 0)
    ci_ = lax.broadcasted_iota(jnp.int32, (c_len, c_len), 1)
    keep = (ri >= ci_, ri <= ci_)
    unroll = True if nc <= 4 else 8

    def chunk_products(c, carry):
        rows = pl.ds(pl.multiple_of(c * c_len, c_len), c_len)
        vt = ia_ref[rows, :].T.astype(BF16)
        for d in range(2):
            scores = jnp.where(keep[d], _dot_nt(ops_scr[d, 0, rows, :], ops_scr[d, 1, rows, :]), 0.0)
            p_scr[d, rows, :] = scores.astype(BF16)
            kv_scr[d, c] = _dot(vt, ops_scr[d, 3, rows, :])
        return carry

    lax.fori_loop(0, nc, chunk_products, 0, unroll=unroll)

    for d in range(2):
        st_scr[d] = s0_ref[d].T

    def scan_step(c, carry):
        for d in range(2):
            cidx = c if d == 0 else nc - 1 - c
            rows = pl.ds(pl.multiple_of(cidx * c_len, c_len), c_len)
            st = st_scr[d]
            o_scr[d, rows, :] = (_dot(p_scr[d, rows, :], ia_ref[rows, :].astype(BF16))
                                 + _dot_nt(ops_scr[d, 2, rows, :], st.astype(BF16)))
            st_scr[d] = dec_scr[d, cidx][0:1, :] * st + kv_scr[d, cidx]
        return carry

    lax.fori_loop(0, nc, scan_step, 0, unroll=unroll)
    if emit_state:
        for d in range(2):
            snew_ref[d] = st_scr[d].T

    def finish(i, carry):
        rows = pl.ds(pl.multiple_of(i * rb, rb), rb)
        o = _rms(o_scr[0, rows, :] + o_scr[1, rows, :], gn_ref[...]) * _silu(ga_ref[rows, :])
        out_ref[rows, :] = o.astype(out_ref.dtype)
        return carry

    lax.fori_loop(0, n // rb, finish, 0)


def _gla_kernel(qa_ref, fzf_ref, fzb_ref, ia_ref, ga_ref, lbp_ref, gn_ref, s0_ref, out_ref, *rest,
                n, emit_state, heads):
    for hh in range(heads):
        cols = pl.ds(hh * A_DK, A_DK)
        head_rest = (rest[0].at[:, hh],) + tuple(rest[1:]) if emit_state else rest
        _gla_head(qa_ref.at[:, cols], fzf_ref.at[:, cols], fzb_ref.at[:, cols], ia_ref.at[:, cols],
                  ga_ref.at[:, cols], lbp_ref.at[:, :, hh], gn_ref, s0_ref.at[:, hh],
                  out_ref.at[:, cols], *head_rest, n=n, emit_state=emit_state)


def _gla(proj3, hgrn_lb, gnorm, s0, emit_state):
    bsz, n, _ = proj3.shape
    h = A_HEADS
    g = max(GLA_HEADS_PER_STEP, min(h, GLA_ROWS_PER_STEP // n))
    hg = h // g

    def col(k):
        return pl.BlockSpec((None, n, g * A_DK), lambda b, hh: (b, 0, k * hg + hh))

    n_lb = hgrn_lb.shape[1]
    lb_spec = pl.BlockSpec((2, n_lb, g, 1, A_DK), lambda b, hh: (0, 0, hh, 0, 0))
    st_spec = pl.BlockSpec((None, 2, g, A_DK, A_DV), lambda b, hh: (b, 0, hh, 0, 0))
    out_shape = [jax.ShapeDtypeStruct((bsz, n, A_WIDTH), BF16)]
    out_specs = [pl.BlockSpec((None, n, g * A_DV), lambda b, hh: (b, 0, hh))]
    if emit_state:
        out_shape.append(jax.ShapeDtypeStruct((bsz, 2, h, A_DK, A_DV), F32))
        out_specs.append(st_spec)
    res = pl.pallas_call(
        functools.partial(_gla_kernel, n=n, emit_state=emit_state, heads=g),
        grid=(bsz, hg),
        in_specs=[col(0), col(1), col(2), col(3), col(4), lb_spec,
                  pl.BlockSpec((1, A_DV), lambda b, hh: (0, 0)), st_spec],
        out_specs=out_specs,
        out_shape=out_shape,
        scratch_shapes=[pltpu.VMEM((2, 4, n, A_DK), BF16),
                        pltpu.VMEM((2, n // SCAN_CHUNK, 8, A_DK), F32),
                        pltpu.VMEM((2, n, SCAN_CHUNK), BF16),
                        pltpu.VMEM((2, n // SCAN_CHUNK, A_DV, A_DK), F32),
                        pltpu.VMEM((2, n, A_DV), F32),
                        pltpu.VMEM((2, A_DV, A_DK), F32)],
        compiler_params=_params("parallel", "parallel"),
        name="hgrn2",
    )(proj3, proj3, proj3, proj3, proj3, hgrn_lb.reshape(2, n_lb, h, 1, A_DK),
      gnorm.reshape(1, A_DV), s0)
    return res


def _gmlp_kernel(ub_ref, vb_ref, vn_ref, ws_ref, bst_ref, out_ref, *, rows):
    for g in range(B_GROUPS):
        cs = slice(g * B_CG, (g + 1) * B_CG)
        vv = _rms(_gelu_tanh(vb_ref[:, cs]), vn_ref[:, cs]).astype(BF16)
        w = ws_ref[g].astype(BF16)
        bias = bst_ref[:, g:g + 1]
        for c in range(rows // B_CHUNK):
            rs = slice(c * B_CHUNK, (c + 1) * B_CHUNK)
            mixed = _dot(w, vv[rs, :]) + bias
            out_ref[rs, cs] = (_gelu_tanh(ub_ref[rs, cs]) * mixed).astype(out_ref.dtype)


def _gmlp(proj, vnorm, ws, bs, rows):
    m = proj.shape[0]
    ub_blk = 5 * A_WIDTH // B_WIDTH
    return pl.pallas_call(
        functools.partial(_gmlp_kernel, rows=rows),
        grid=(m // rows,),
        in_specs=[
            pl.BlockSpec((rows, B_WIDTH), lambda i: (i, ub_blk)),
            pl.BlockSpec((rows, B_WIDTH), lambda i: (i, ub_blk + 1)),
            pl.BlockSpec((1, B_WIDTH), lambda i: (0, 0)),
            pl.BlockSpec((B_GROUPS, B_CHUNK, B_CHUNK), lambda i: (0, 0, 0)),
            pl.BlockSpec((B_CHUNK, B_GROUPS), lambda i: (0, 0)),
        ],
        out_specs=pl.BlockSpec((rows, B_WIDTH), lambda i: (i, 0)),
        out_shape=jax.ShapeDtypeStruct((m, B_WIDTH), BF16),
        compiler_params=_params("parallel"),
        name="gmlp",
    )(proj, proj, vnorm.reshape(1, B_WIDTH), ws, bs.T)


def _resid_proj_kernel(*refs, n_parts):
    x_ref, gate_ref = refs[0], refs[1]
    a_refs = refs[2:2 + n_parts]
    w_refs = refs[2 + n_parts:2 + 2 * n_parts]
    o_ref = refs[2 + 2 * n_parts]
    acc = _dot(a_refs[0][...], w_refs[0][...])
    for a_ref, w_ref in zip(a_refs[1:], w_refs[1:]):
        acc = acc + _dot(a_ref[...], w_ref[...])
    o_ref[...] = x_ref[...] + gate_ref[...] * acc


def _resid_proj(x, mod4, gate_idx, parts, w, rows_per_mod, mod_off, tm):
    m = x.shape[0]
    n_parts = len(parts)
    kp = parts[0].shape[1]
    in_specs = [pl.BlockSpec((tm, D_MODEL), lambda i: (i, 0)),
                _mod_spec(gate_idx, tm, rows_per_mod, mod_off)]
    in_specs += [pl.BlockSpec((tm, kp), lambda i: (i, 0)) for _ in parts]
    in_specs += [pl.BlockSpec((kp, D_MODEL), lambda i, p=p: (p, 0)) for p in range(n_parts)]
    return pl.pallas_call(
        functools.partial(_resid_proj_kernel, n_parts=n_parts),
        grid=(m // tm,),
        in_specs=in_specs,
        out_specs=pl.BlockSpec((tm, D_MODEL), lambda i: (i, 0)),
        out_shape=jax.ShapeDtypeStruct((m, D_MODEL), F32),
        compiler_params=_params("parallel"),
        name="resid_proj",
    )(x, mod4, *parts, *([w] * n_parts))


def _dft_tables(n):
    idx = np.arange(n, dtype=np.int64)
    ang = 2.0 * np.pi * ((idx[:, None] * idx[None, :]) % n).astype(np.float64) / n
    s = 1.0 / np.sqrt(n)
    return (np.cos(ang) * s).astype(np.float32), (np.sin(ang) * s).astype(np.float32)


def _norm_chan_dft_kernel(x_ref, g_ref, sc_ref, sh_ref, cc_ref, sn_ref, p_ref, q_ref, h_scr):
    _norm_mod_rows(x_ref, g_ref, sc_ref, sh_ref, h_scr)
    for g in range(C_GROUPS):
        cs = slice(g * C_CG, (g + 1) * C_CG)
        p_ref[:, cs] = _dot(h_scr[:, cs], cc_ref[...]).astype(p_ref.dtype)
        q_ref[:, cs] = _dot(h_scr[:, cs], sn_ref[...]).astype(q_ref.dtype)


def _norm_chan_dft(x, g, mod4, sc_idx, sh_idx, rows_per_mod, mod_off, tm):
    m = x.shape[0]
    cc, sn = _dft_tables(C_CG)
    cc = jnp.asarray(cc).astype(BF16)
    sn = jnp.asarray(sn).astype(BF16)
    tab = pl.BlockSpec((C_CG, C_CG), lambda i: (0, 0))
    out = pl.BlockSpec((tm, D_MODEL), lambda i: (i, 0))
    return pl.pallas_call(
        _norm_chan_dft_kernel,
        grid=(m // tm,),
        in_specs=[
            pl.BlockSpec((tm, D_MODEL), lambda i: (i, 0)),
            pl.BlockSpec((1, D_MODEL), lambda i: (0, 0)),
            _mod_spec(sc_idx, tm, rows_per_mod, mod_off),
            _mod_spec(sh_idx, tm, rows_per_mod, mod_off),
            tab, tab,
        ],
        out_specs=[out, out],
        out_shape=[jax.ShapeDtypeStruct((m, D_MODEL), BF16)] * 2,
        scratch_shapes=[pltpu.VMEM((tm, D_MODEL), BF16)],
        compiler_params=_params("parallel"),
        name="norm_chan_dft",
    )(x, g.reshape(1, D_MODEL), mod4, mod4, cc, sn)


def _pos_dft_kernel(cn_ref, sn_ref, p_ref, q_ref, o_ref):
    o_ref[...] = (_dot(cn_ref[...], p_ref[...]) - _dot(sn_ref[...], q_ref[...])).astype(o_ref.dtype)


def _pos_dft(p3, q3, tn):
    bsz, n, _ = p3.shape
    cn, sn = _dft_tables(n)
    cn = jnp.asarray(cn).astype(BF16)
    sn = jnp.asarray(sn).astype(BF16)
    tab = pl.BlockSpec((n, n), lambda b, j: (0, 0))
    blk = pl.BlockSpec((None, n, tn), lambda b, j: (b, 0, j))
    return pl.pallas_call(
        _pos_dft_kernel,
        grid=(bsz, D_MODEL // tn),
        in_specs=[tab, tab, blk, blk],
        out_specs=blk,
        out_shape=jax.ShapeDtypeStruct((bsz, n, D_MODEL), BF16),
        compiler_params=_params("parallel", "parallel"),
        name="pos_dft",
    )(cn, sn, p3, q3)


FFN_SCHED = {name: i for i, name in enumerate((
    "up_rows", "up_mod", "up_col", "conv_col", "down_rows", "down_mod", "down_col",
    "norm_due", "reset_due", "out_due"))}


def _ffn_kernel(sched_ref, xc_ref, gf_ref, modu_ref, xp_ref, modd_ref, wgv_ref, cp_ref, wd_ref,
                mask_ref, o_ref, h_scr, acc_scr, u0_scr, u1_scr, act0_scr, act1_scr, *,
                final_norm, pieces):
    t = pl.program_id(0)
    tm, tf = act0_scr.shape
    lanes = mask_ref.shape[2]

    def due(name):
        return sched_ref[FFN_SCHED[name], t] != 0

    @pl.when(t == 0)
    def _():
        for ref in (u0_scr, u1_scr, act0_scr, act1_scr, acc_scr):
            ref[...] = jnp.zeros_like(ref)

    @pl.when(due("norm_due"))
    def _():
        _norm_mod_rows(xc_ref, gf_ref.at[0], modu_ref.at[1], modu_ref.at[0], h_scr)

    @pl.when(due("reset_due"))
    def _():
        acc_scr[...] = jnp.zeros_like(acc_scr)

    cb_rows = FFN_CONV_ROWS
    n_row_blocks = tm // cb_rows
    mask_rows = mask_ref.shape[1]

    def conv(u_old, k, r, cs):
        p0 = 4 * k
        ext = u_old[k, r * cb_rows:(r + 1) * cb_rows + 2 * HALO, cs]
        n_ext = cb_rows + 2 * HALO
        ms = slice((r * cb_rows) % mask_rows, (r * cb_rows) % mask_rows + cb_rows)
        inner = slice(HALO, HALO + cb_rows)
        prev = pltpu.roll(ext, 1, axis=0)[inner] * mask_ref[0, ms, :]
        nxt = pltpu.roll(ext, n_ext - 1, axis=0)[inner] * mask_ref[1, ms, :]
        return (prev * cp_ref[p0:p0 + 1, cs] + ext[inner] * cp_ref[p0 + 1:p0 + 2, cs]
                + nxt * cp_ref[p0 + 2:p0 + 3, cs] + cp_ref[p0 + 3:p0 + 4, cs])

    def conv_block(u_old, act_new, idx):
        c, r = divmod(idx, n_row_blocks)
        cs = slice(c * lanes, (c + 1) * lanes)
        gate = conv(u_old, 0, r, cs)
        val = conv(u_old, 1, r, cs)
        act_new[r * cb_rows:(r + 1) * cb_rows, cs] = (_silu(gate) * val).astype(BF16)

    def stages(u_new, u_old, act_new, act_old):
        n_blocks = (tf // lanes) * n_row_blocks
        up_pieces, down_pieces = pieces
        un = 2 * tf // up_pieces
        dn = D_MODEL // down_pieces
        per_half = tf // un
        h = h_scr[...]
        act = act_old[...]
        mxu = ([("up", q, un * D_MODEL) for q in range(up_pieces)]
               + [("down", q, dn * tf) for q in range(down_pieces)])
        early = n_blocks - FFN_TAIL_BLOCKS
        total = sum(w for _, _, w in mxu[:-1])
        done = 0
        spent = 0
        for n, (kind, q, w) in enumerate(mxu):
            if kind == "down":
                ns = slice(q * dn, (q + 1) * dn)
                acc_scr[:, ns] += _dot(act, wd_ref[:, ns])
            else:
                k, qq = divmod(q, per_half)
                ns = slice(qq * un, (qq + 1) * un)
                u_new[k, HALO:HALO + tm, ns] = _dot(h, wgv_ref[k, :, ns])
            spent += w
            target = n_blocks if n == len(mxu) - 1 else min(early, -(-early * spent // total))
            for idx in range(done, target):
                conv_block(u_old, act_new, idx)
            done = target

    @pl.when(t % 2 == 0)
    def _():
        stages(u0_scr, u1_scr, act1_scr, act0_scr)

    @pl.when(t % 2 == 1)
    def _():
        stages(u1_scr, u0_scr, act0_scr, act1_scr)

    @pl.when(due("out_due"))
    def _():
        rb = NORM_ROW_BLOCK

        def body(i, carry):
            rows = pl.ds(pl.multiple_of(i * rb, rb), rb)
            y = xp_ref[rows, :] + modd_ref[2] * acc_scr[rows, :]
            if final_norm:
                y = _rms(y, gf_ref[1])
            o_ref[rows, :] = y
            acc_scr[rows, :] = jnp.zeros((rb, D_MODEL), F32)
            return carry

        lax.fori_loop(0, tm // rb, body, 0, unroll=NORM_UNROLL)


def _ffn_conv_params(cw, cb, tf):
    cp = jnp.concatenate([cw[:, :D_FF], cb[None, :D_FF], cw[:, D_FF:], cb[None, D_FF:]], axis=0)
    return cp.reshape(8, D_FF // tf, tf).transpose(1, 0, 2)


def _conv_ffn(x, g, mod4, w_up, conv_p, w_down, fin, rows_per_mod, mod_off, row_len, tm,
              final_norm, pieces):
    m = x.shape[0]
    nf, _, tf = conv_p.shape
    n_tiles = (m // tm) * nf
    rb = max(row_len, FFN_CONV_ROWS)
    assert rb % row_len == 0 and rb % FFN_CONV_ROWS == 0 and tm % rb == 0
    pos = np.arange(rb) % row_len
    mask = np.stack([np.broadcast_to((pos != 0)[:, None], (rb, 128)),
                     np.broadcast_to((pos != row_len - 1)[:, None], (rb, 128))]).astype(np.float32)

    t = np.arange(n_tiles + 2)
    up = np.minimum(t, n_tiles - 1)
    cv = np.clip(t - 1, 0, n_tiles - 1)
    dn = np.clip(t - 2, 0, n_tiles - 1)
    sched = np.zeros((len(FFN_SCHED), n_tiles + 2), np.int32)
    sched[FFN_SCHED["up_rows"]] = up // nf
    sched[FFN_SCHED["up_mod"]] = mod_off + (up // nf * tm) // rows_per_mod
    sched[FFN_SCHED["up_col"]] = up % nf
    sched[FFN_SCHED["conv_col"]] = cv % nf
    sched[FFN_SCHED["down_rows"]] = dn // nf
    sched[FFN_SCHED["down_mod"]] = mod_off + (dn // nf * tm) // rows_per_mod
    sched[FFN_SCHED["down_col"]] = dn % nf
    sched[FFN_SCHED["norm_due"]] = (t < n_tiles) & (t % nf == 0)
    sched[FFN_SCHED["reset_due"]] = t == 2
    sched[FFN_SCHED["out_due"]] = (t >= 2) & ((t - 2) % nf == nf - 1)

    def row(name):
        return FFN_SCHED[name]

    def mod_spec(name):
        return pl.BlockSpec((None, 3, 1, D_MODEL), lambda t, s: (s[row(name), t], 1, 0, 0))

    grid_spec = pltpu.PrefetchScalarGridSpec(
        num_scalar_prefetch=1,
        grid=(n_tiles + 2,),
        in_specs=[
            pl.BlockSpec((tm, D_MODEL), lambda t, s: (s[row("up_rows"), t], 0)),
            pl.BlockSpec((2, 1, D_MODEL), lambda t, s: (0, 0, 0)),
            mod_spec("up_mod"),
            pl.BlockSpec((tm, D_MODEL), lambda t, s: (s[row("down_rows"), t], 0)),
            mod_spec("down_mod"),
            pl.BlockSpec((2, D_MODEL, tf), lambda t, s: (s[row("up_col"), t], 0, 0)),
            pl.BlockSpec((None, 8, tf), lambda t, s: (s[row("conv_col"), t], 0, 0)),
            pl.BlockSpec((tf, D_MODEL), lambda t, s: (s[row("down_col"), t], 0)),
            pl.BlockSpec((2, rb, 128), lambda t, s: (0, 0, 0)),
        ],
        out_specs=pl.BlockSpec((tm, D_MODEL), lambda t, s: (s[row("down_rows"), t], 0)),
        scratch_shapes=[pltpu.VMEM((tm, D_MODEL), BF16), pltpu.VMEM((tm, D_MODEL), F32),
                        pltpu.VMEM((2, tm + 2 * HALO, tf), F32),
                        pltpu.VMEM((2, tm + 2 * HALO, tf), F32),
                        pltpu.VMEM((tm, tf), BF16), pltpu.VMEM((tm, tf), BF16)],
    )
    return pl.pallas_call(
        functools.partial(_ffn_kernel, final_norm=final_norm, pieces=pieces),
        grid_spec=grid_spec,
        out_shape=jax.ShapeDtypeStruct((m, D_MODEL), F32),
        compiler_params=_params("arbitrary"),
        name="conv_ffn",
    )(jnp.asarray(sched), x, jnp.stack([g, fin]).reshape(2, 1, D_MODEL), mod4, x, mod4, w_up, conv_p,
      w_down, jnp.asarray(mask))


def kernel(x_prompt, x_sample, state_l0_hgrn, c, c_ctx, mod_w_0, mod_b_0, norm1_0, w_in_0, hgrn_lb, hgrn_gnorm_0, gmlp_vnorm_0, gmlp_ws_0, gmlp_bs_0, w_out_0, norm2_0, ffn_up_0, ffn_conv_w_0, ffn_conv_b_0, ffn_down_0, mod_w_1, mod_b_1, norm1_1, w_out_1, norm2_1, ffn_up_1, ffn_conv_w_1, ffn_conv_b_1, ffn_down_1, final_norm):
    n_dec = c.shape[0]
    cond = jnp.concatenate(
        [c, c_ctx[None, :], jnp.zeros((MOD_ROWS - n_dec - 1, D_MODEL), F32)], axis=0)
    mod4 = [_modulation(cond, w, b).reshape(MOD_ROWS, 6, 1, D_MODEL)
            for w, b in ((mod_w_0, mod_b_0), (mod_w_1, mod_b_1))]

    w_in = _cast_col_tiles(w_in_0, PROJ_COL_TILE)
    w_out0 = _cast_rows(w_out_0, CAST_ROWS)
    w_out1 = _cast_rows(w_out_1, CAST_ROWS)
    ffn = [(norm2_0, _cast_col_tiles(ffn_up_0, FFN_TF, pair_halves=True), _ffn_conv_params(ffn_conv_w_0, ffn_conv_b_0, FFN_TF),
            _cast_rows(ffn_down_0, CAST_ROWS)),
           (norm2_1, _cast_col_tiles(ffn_up_1, FFN_TF, pair_halves=True), _ffn_conv_params(ffn_conv_w_1, ffn_conv_b_1, FFN_TF),
            _cast_rows(ffn_down_1, CAST_ROWS))]

    def trunk(x3, s0, mod_off, per_batch_mod, row_len, emit_state):
        bsz, n, _ = x3.shape
        m = bsz * n
        rows_per_mod = n if per_batch_mod else m
        tm = ROW_TILE
        x = x3.reshape(m, D_MODEL)

        proj = _norm_proj(x, norm1_0, mod4[0], 1, 0, w_in, rows_per_mod, mod_off, PROJ_ROW_TILE, F32)
        res = _gla(proj.reshape(bsz, n, IN_WIDTH), hgrn_lb, hgrn_gnorm_0, s0,
                   emit_state)
        out_b = _gmlp(proj, gmlp_vnorm_0, gmlp_ws_0, gmlp_bs_0, GMLP_ROWS)
        x = _resid_proj(x, mod4[0], 2, [res[0].reshape(m, A_WIDTH), out_b], w_out0,
                        rows_per_mod, mod_off, tm)
        x = _conv_ffn(x, ffn[0][0], mod4[0], *ffn[0][1:], final_norm, rows_per_mod, mod_off,
                      row_len, tm, False, FFN_MXU_PIECES)

        p, q = _norm_chan_dft(x, norm1_1, mod4[1], 1, 0, rows_per_mod, mod_off, tm)
        four = _pos_dft(p.reshape(bsz, n, D_MODEL), q.reshape(bsz, n, D_MODEL), POS_DFT_COL_TILE)
        x = _resid_proj(x, mod4[1], 2, [four.reshape(m, D_MODEL)], w_out1, rows_per_mod, mod_off, tm)
        x = _conv_ffn(x, ffn[1][0], mod4[1], *ffn[1][1:], final_norm, rows_per_mod, mod_off,
                      row_len, tm, True, FFN_MXU_PIECES)
        return x.reshape(bsz, n, D_MODEL), (res[1] if emit_state else None)

    zero_state = jnp.zeros((x_prompt.shape[0], 2, A_HEADS, A_DK, A_DV), F32)
    y_prompt, state_new = trunk(x_prompt, zero_state, n_dec, False, x_prompt.shape[1], True)
    y_sample, _ = trunk(x_sample, state_l0_hgrn, 0, True, GRID_W, False)
    return (y_prompt, y_sample, state_new.astype(x_prompt.dtype))
```

```python
import functools

import numpy as np
import jax
import jax.numpy as jnp
from jax import lax
from jax.experimental import pallas as pl
from jax.experimental.pallas import tpu as pltpu

D_MODEL = 2048
A_WIDTH = 1024
A_HEADS = 8
A_DK = 128
A_DV = 128
B_WIDTH = 1024
B_GROUPS = 4
B_CHUNK = 128
B_CG = B_WIDTH // B_GROUPS
C_GROUPS = 4
C_CG = D_MODEL // C_GROUPS
SCAN_CHUNK = 64
D_FF = 5632
GRID_W = 64
IN_WIDTH = 5 * A_WIDTH + 2 * B_WIDTH
EPS = 1e-6

F32 = jnp.float32
BF16 = jnp.bfloat16

VMEM_LIMIT = 56 * 1024 * 1024
MOD_ROWS = 16
GLA_ROW_BLOCK = 256
GLA_HEADS_PER_STEP = 2
GLA_SCAN_UNROLL = 8
GLA_ROWS_PER_STEP = 2048
FFN_CONV_ROWS = 64
FFN_MXU_PIECES = (4, 8)
FFN_TAIL_BLOCKS = 1
HALO = 8
NORM_ROW_BLOCK = 16
NORM_UNROLL = 8
ROW_TILE = 512
FFN_TF = 512
PROJ_ROW_TILE = 1024
PROJ_COL_TILE = 1792
POS_DFT_COL_TILE = 1024
CAST_ROWS = 512
GMLP_ROWS = 1024


def _params(*sem):
    return pltpu.CompilerParams(dimension_semantics=sem, vmem_limit_bytes=VMEM_LIMIT)


def _sigmoid(x):
    return 1.0 / (1.0 + jnp.exp(-x))


def _silu(x):
    return x * _sigmoid(x)


def _gelu_tanh(x):
    c = np.float32(np.sqrt(2.0 / np.pi))
    return 0.5 * x * (1.0 + jnp.tanh(c * (x + 0.044715 * (x * x * x))))


def _rms(x, g):
    return x * lax.rsqrt(jnp.mean(x * x, axis=-1, keepdims=True) + EPS) * g


def _norm_mod_rows(x_ref, g_ref, sc_ref, sh_ref, h_ref):
    rb = NORM_ROW_BLOCK

    def body(i, carry):
        rows = pl.ds(pl.multiple_of(i * rb, rb), rb)
        x = x_ref[rows, :]
        y = x * lax.rsqrt(jnp.mean(x * x, axis=-1, keepdims=True) + EPS)
        h_ref[rows, :] = (y * (g_ref[...] * (1.0 + sc_ref[...])) + sh_ref[...]).astype(h_ref.dtype)
        return carry

    lax.fori_loop(0, x_ref.shape[0] // rb, body, 0, unroll=NORM_UNROLL)


def _dot(a, b):
    return jnp.dot(a, b, preferred_element_type=F32)


def _dot_nt(a, b):
    return lax.dot_general(a, b, (((1,), (1,)), ((), ())), preferred_element_type=F32)


def _cast_kernel(w_ref, o_ref):
    o_ref[...] = w_ref[...].astype(o_ref.dtype)


def _cast_col_tiles(w, tn, pair_halves=False):
    k, n = w.shape
    half = n // tn // 2
    slot = (lambda j: jnp.where(j < half, 2 * j, 2 * (j - half) + 1)) if pair_halves else (lambda j: j)
    return pl.pallas_call(
        _cast_kernel,
        grid=(n // tn,),
        in_specs=[pl.BlockSpec((k, tn), lambda j: (0, j))],
        out_specs=pl.BlockSpec((None, k, tn), lambda j: (slot(j), 0, 0)),
        out_shape=jax.ShapeDtypeStruct((n // tn, k, tn), BF16),
        compiler_params=_params("parallel"),
        name="cast_col_tiles",
    )(w)


def _cast_rows(w, tk):
    k, n = w.shape
    return pl.pallas_call(
        _cast_kernel,
        grid=(k // tk,),
        in_specs=[pl.BlockSpec((tk, n), lambda i: (i, 0))],
        out_specs=pl.BlockSpec((tk, n), lambda i: (i, 0)),
        out_shape=jax.ShapeDtypeStruct((k, n), BF16),
        compiler_params=_params("parallel"),
        name="cast_rows",
    )(w)


def _mod_kernel(c_ref, w_ref, b_ref, o_ref):
    s = _silu(c_ref[...]).astype(BF16)
    o_ref[...] = _dot(s, w_ref[...].astype(BF16)) + b_ref[...]


def _modulation(cond, w, b):
    tn = 1024
    n = w.shape[1]
    return pl.pallas_call(
        _mod_kernel,
        grid=(n // tn,),
        in_specs=[
            pl.BlockSpec((MOD_ROWS, D_MODEL), lambda j: (0, 0)),
            pl.BlockSpec((D_MODEL, tn), lambda j: (0, j)),
            pl.BlockSpec((1, tn), lambda j: (0, j)),
        ],
        out_specs=pl.BlockSpec((MOD_ROWS, tn), lambda j: (0, j)),
        out_shape=jax.ShapeDtypeStruct((MOD_ROWS, n), F32),
        compiler_params=_params("arbitrary"),
        name="modulation",
    )(cond, w, b.reshape(1, n))


def _mod_spec(which, tm, rows_per_mod, mod_off):
    return pl.BlockSpec(
        (None, None, 1, D_MODEL),
        lambda i, *_: (mod_off + (i * tm) // rows_per_mod, which, 0, 0))


def _norm_proj_kernel(x_ref, g_ref, sc_ref, sh_ref, w_ref, o_ref, h_scr):
    @pl.when(pl.program_id(1) == 0)
    def _():
        _norm_mod_rows(x_ref, g_ref, sc_ref, sh_ref, h_scr)

    o_ref[...] = _dot(h_scr[...], w_ref[...]).astype(o_ref.dtype)


def _norm_proj(x, g, mod4, sc_idx, sh_idx, w_tiles, rows_per_mod, mod_off, tm, out_dtype):
    m = x.shape[0]
    nt, _, tn = w_tiles.shape
    n = nt * tn
    return pl.pallas_call(
        _norm_proj_kernel,
        grid=(m // tm, n // tn),
        in_specs=[
            pl.BlockSpec((tm, D_MODEL), lambda i, j: (i, 0)),
            pl.BlockSpec((1, D_MODEL), lambda i, j: (0, 0)),
            _mod_spec(sc_idx, tm, rows_per_mod, mod_off),
            _mod_spec(sh_idx, tm, rows_per_mod, mod_off),
            pl.BlockSpec((None, D_MODEL, tn), lambda i, j: (j, 0, 0)),
        ],
        out_specs=pl.BlockSpec((tm, tn), lambda i, j: (i, j)),
        out_shape=jax.ShapeDtypeStruct((m, n), out_dtype),
        scratch_shapes=[pltpu.VMEM((tm, D_MODEL), BF16)],
        compiler_params=_params("parallel", "arbitrary"),
        name="norm_proj",
    )(x, g.reshape(1, D_MODEL), mod4, mod4, w_tiles)


def _gla_head(qa_ref, fzf_ref, fzb_ref, ia_ref, ga_ref, lbp_ref, gn_ref, s0_ref,
              out_ref, *rest, n, emit_state):
    if emit_state:
        snew_ref, ops_scr, dec_scr, p_scr, kv_scr, o_scr, st_scr = rest
    else:
        ops_scr, dec_scr, p_scr, kv_scr, o_scr, st_scr = rest
    c_len = SCAN_CHUNK
    nc = n // c_len
    mid = c_len // 2
    rb = GLA_ROW_BLOCK
    cpb = rb // c_len

    def lower_bound(d):
        a = [lbp_ref[d, l] for l in range(lbp_ref.shape[1])]
        mx = functools.reduce(jnp.maximum, a)
        e = [jnp.exp(t - mx) for t in a]
        return e[0] / functools.reduce(lambda u, w: u + w, e)

    lbs = (lower_bound(0), lower_bound(1))
    pos = lax.broadcasted_iota(jnp.int32, (rb, A_DK), 0) % c_len

    def operands(i):
        r0 = pl.multiple_of(i * rb, rb)
        rows = pl.ds(r0, rb)
        q = _silu(qa_ref[rows, :]) * (A_DK ** -0.5)
        for d, fz_ref in enumerate((fzf_ref, fzb_ref)):
            lb = lbs[d]
            f = lb + (1.0 - lb) * _sigmoid(fz_ref[rows, :])
            k = 1.0 - f
            b = jnp.log2(f)
            for s in (1, 2, 4, 8, 16, 32):
                if d == 0:
                    b = b + jnp.where(pos >= s, pltpu.roll(b, s, axis=0), 0.0)
                else:
                    b = b + jnp.where(pos < c_len - s, pltpu.roll(b, rb - s, axis=0), 0.0)
            for ci in range(cpb):
                sl = slice(ci * c_len, (ci + 1) * c_len)
                bc = b[sl]
                if d == 0:
                    ref, b_last = bc[mid - 1:mid], bc[c_len - 1:c_len]
                else:
                    ref, b_last = bc[c_len - mid:c_len - mid + 1], bc[0:1]
                qe = q[sl] * jnp.exp2(bc - ref)
                ke = k[sl] * jnp.exp2(ref - bc)
                crow = pl.ds(r0 + ci * c_len, c_len)
                ops_scr[d, 0, crow, :] = qe.astype(BF16)
                ops_scr[d, 1, crow, :] = ke.astype(BF16)
                ops_scr[d, 2, crow, :] = (qe * jnp.exp2(ref)).astype(BF16)
                ops_scr[d, 3, crow, :] = (ke * jnp.exp2(b_last - ref)).astype(BF16)
                dec_scr[d, i * cpb + ci] = jnp.broadcast_to(jnp.exp2(b_last), (8, A_DK))

    ri = lax.broadcasted_iota(jnp.int32, (c_len, c_len), 0)
    ci_ = lax.broadcasted_iota(jnp.int32, (c_len, c_len), 1)
    keep = (ri >= ci_, ri <= ci_)

    def chunk_products(c):
        rows = pl.ds(pl.multiple_of(c * c_len, c_len), c_len)
        vt = ia_ref[rows, :].T.astype(BF16)
        for d in range(2):
            scores = jnp.where(keep[d], _dot_nt(ops_scr[d, 0, rows, :], ops_scr[d, 1, rows, :]), 0.0)
            p_scr[d, rows, :] = scores.astype(BF16)
            kv_scr[d, c] = _dot(vt, ops_scr[d, 3, rows, :])

    def init_state():
        for d in range(2):
            st_scr[d] = s0_ref[d].T

    def scan_step(c):
        for d in range(2):
            cidx = c if d == 0 else nc - 1 - c
            rows = pl.ds(pl.multiple_of(cidx * c_len, c_len), c_len)
            st = st_scr[d]
            o_scr[d, rows, :] = (_dot(p_scr[d, rows, :], ia_ref[rows, :].astype(BF16))
                                 + _dot_nt(ops_scr[d, 2, rows, :], st.astype(BF16)))
            st_scr[d] = dec_scr[d, cidx][0:1, :] * st + kv_scr[d, cidx]

    def write_state():
        if emit_state:
            for d in range(2):
                snew_ref[d] = st_scr[d].T

    def finish(i):
        rows = pl.ds(pl.multiple_of(i * rb, rb), rb)
        o = _rms(o_scr[0, rows, :] + o_scr[1, rows, :], gn_ref[...]) * _silu(ga_ref[rows, :])
        out_ref[rows, :] = o.astype(out_ref.dtype)

    return operands, chunk_products, init_state, scan_step, write_state, finish


def _gla_kernel(qa_ref, fzf_ref, fzb_ref, ia_ref, ga_ref, lbp_ref, gn_ref, s0_ref, out_ref, *rest,
                n, emit_state, heads):
    snew_ref = rest[0] if emit_state else None
    scratch = rest[1:] if emit_state else rest
    phases = []
    for hh in range(heads):
        cols = pl.ds(hh * A_DK, A_DK)
        head_rest = tuple(r.at[hh] for r in scratch)
        if emit_state:
            head_rest = (snew_ref.at[:, hh],) + head_rest
        phases.append(_gla_head(
            qa_ref.at[:, cols], fzf_ref.at[:, cols], fzb_ref.at[:, cols], ia_ref.at[:, cols],
            ga_ref.at[:, cols], lbp_ref.at[:, :, hh], gn_ref, s0_ref.at[:, hh],
            out_ref.at[:, cols], *head_rest, n=n, emit_state=emit_state))
    operands, chunk_products, init_state, scan_step, write_state, finish = zip(*phases)
    nc = n // SCAN_CHUNK
    unroll = True if nc <= 4 else GLA_SCAN_UNROLL

    def each(fns):
        def body(i, carry):
            for fn in fns:
                fn(i)
            return carry
        return body

    lax.fori_loop(0, n // GLA_ROW_BLOCK, each(operands), 0)
    lax.fori_loop(0, nc, each(chunk_products), 0, unroll=unroll)
    for fn in init_state:
        fn()
    lax.fori_loop(0, nc, each(scan_step), 0, unroll=unroll)
    for fn in write_state:
        fn()
    lax.fori_loop(0, n // GLA_ROW_BLOCK, each(finish), 0)


def _gla(proj3, hgrn_lb, gnorm, s0, emit_state):
    bsz, n, _ = proj3.shape
    h = A_HEADS
    g = max(GLA_HEADS_PER_STEP, min(h, GLA_ROWS_PER_STEP // n))
    hg = h // g

    def col(k):
        return pl.BlockSpec((None, n, g * A_DK), lambda b, hh: (b, 0, k * hg + hh))

    n_lb = hgrn_lb.shape[1]
    lb_spec = pl.BlockSpec((2, n_lb, g, 1, A_DK), lambda b, hh: (0, 0, hh, 0, 0))
    st_spec = pl.BlockSpec((None, 2, g, A_DK, A_DV), lambda b, hh: (b, 0, hh, 0, 0))
    out_shape = [jax.ShapeDtypeStruct((bsz, n, A_WIDTH), BF16)]
    out_specs = [pl.BlockSpec((None, n, g * A_DV), lambda b, hh: (b, 0, hh))]
    if emit_state:
        out_shape.append(jax.ShapeDtypeStruct((bsz, 2, h, A_DK, A_DV), F32))
        out_specs.append(st_spec)
    res = pl.pallas_call(
        functools.partial(_gla_kernel, n=n, emit_state=emit_state, heads=g),
        grid=(bsz, hg),
        in_specs=[col(0), col(1), col(2), col(3), col(4), lb_spec,
                  pl.BlockSpec((1, A_DV), lambda b, hh: (0, 0)), st_spec],
        out_specs=out_specs,
        out_shape=out_shape,
        scratch_shapes=[pltpu.VMEM((g, 2, 4, n, A_DK), BF16),
                        pltpu.VMEM((g, 2, n // SCAN_CHUNK, 8, A_DK), F32),
                        pltpu.VMEM((g, 2, n, SCAN_CHUNK), BF16),
                        pltpu.VMEM((g, 2, n // SCAN_CHUNK, A_DV, A_DK), F32),
                        pltpu.VMEM((g, 2, n, A_DV), F32),
                        pltpu.VMEM((g, 2, A_DV, A_DK), F32)],
        compiler_params=_params("parallel", "parallel"),
        name="hgrn2",
    )(proj3, proj3, proj3, proj3, proj3, hgrn_lb.reshape(2, n_lb, h, 1, A_DK),
      gnorm.reshape(1, A_DV), s0)
    return res


def _gmlp_kernel(ub_ref, vb_ref, vn_ref, ws_ref, bst_ref, out_ref, *, rows):
    for g in range(B_GROUPS):
        cs = slice(g * B_CG, (g + 1) * B_CG)
        vv = _rms(_gelu_tanh(vb_ref[:, cs]), vn_ref[:, cs]).astype(BF16)
        w = ws_ref[g].astype(BF16)
        bias = bst_ref[:, g:g + 1]
        for c in range(rows // B_CHUNK):
            rs = slice(c * B_CHUNK, (c + 1) * B_CHUNK)
            mixed = _dot(w, vv[rs, :]) + bias
            out_ref[rs, cs] = (_gelu_tanh(ub_ref[rs, cs]) * mixed).astype(out_ref.dtype)


def _gmlp(proj, vnorm, ws, bs, rows):
    m = proj.shape[0]
    ub_blk = 5 * A_WIDTH // B_WIDTH
    return pl.pallas_call(
        functools.partial(_gmlp_kernel, rows=rows),
        grid=(m // rows,),
        in_specs=[
            pl.BlockSpec((rows, B_WIDTH), lambda i: (i, ub_blk)),
            pl.BlockSpec((rows, B_WIDTH), lambda i: (i, ub_blk + 1)),
            pl.BlockSpec((1, B_WIDTH), lambda i: (0, 0)),
            pl.BlockSpec((B_GROUPS, B_CHUNK, B_CHUNK), lambda i: (0, 0, 0)),
            pl.BlockSpec((B_CHUNK, B_GROUPS), lambda i: (0, 0)),
        ],
        out_specs=pl.BlockSpec((rows, B_WIDTH), lambda i: (i, 0)),
        out_shape=jax.ShapeDtypeStruct((m, B_WIDTH), BF16),
        compiler_params=_params("parallel"),
        name="gmlp",
    )(proj, proj, vnorm.reshape(1, B_WIDTH), ws, bs.T)


def _resid_proj_kernel(*refs, n_parts):
    x_ref, gate_ref = refs[0], refs[1]
    a_refs = refs[2:2 + n_parts]
    w_refs = refs[2 + n_parts:2 + 2 * n_parts]
    o_ref = refs[2 + 2 * n_parts]
    acc = _dot(a_refs[0][...], w_refs[0][...])
    for a_ref, w_ref in zip(a_refs[1:], w_refs[1:]):
        acc = acc + _dot(a_ref[...], w_ref[...])
    o_ref[...] = x_ref[...] + gate_ref[...] * acc


def _resid_proj(x, mod4, gate_idx, parts, w, rows_per_mod, mod_off, tm):
    m = x.shape[0]
    n_parts = len(parts)
    kp = parts[0].shape[1]
    in_specs = [pl.BlockSpec((tm, D_MODEL), lambda i: (i, 0)),
                _mod_spec(gate_idx, tm, rows_per_mod, mod_off)]
    in_specs += [pl.BlockSpec((tm, kp), lambda i: (i, 0)) for _ in parts]
    in_specs += [pl.BlockSpec((kp, D_MODEL), lambda i, p=p: (p, 0)) for p in range(n_parts)]
    return pl.pallas_call(
        functools.partial(_resid_proj_kernel, n_parts=n_parts),
        grid=(m // tm,),
        in_specs=in_specs,
        out_specs=pl.BlockSpec((tm, D_MODEL), lambda i: (i, 0)),
        out_shape=jax.ShapeDtypeStruct((m, D_MODEL), F32),
        compiler_params=_params("parallel"),
        name="resid_proj",
    )(x, mod4, *parts, *([w] * n_parts))


def _dft_tables(n):
    idx = np.arange(n, dtype=np.int64)
    ang = 2.0 * np.pi * ((idx[:, None] * idx[None, :]) % n).astype(np.float64) / n
    s = 1.0 / np.sqrt(n)
    return (np.cos(ang) * s).astype(np.float32), (np.sin(ang) * s).astype(np.float32)


def _norm_chan_dft_kernel(x_ref, g_ref, sc_ref, sh_ref, cc_ref, sn_ref, p_ref, q_ref, h_scr):
    _norm_mod_rows(x_ref, g_ref, sc_ref, sh_ref, h_scr)
    for g in range(C_GROUPS):
        cs = slice(g * C_CG, (g + 1) * C_CG)
        p_ref[:, cs] = _dot(h_scr[:, cs], cc_ref[...]).astype(p_ref.dtype)
        q_ref[:, cs] = _dot(h_scr[:, cs], sn_ref[...]).astype(q_ref.dtype)


def _norm_chan_dft(x, g, mod4, sc_idx, sh_idx, rows_per_mod, mod_off, tm):
    m = x.shape[0]
    cc, sn = _dft_tables(C_CG)
    cc = jnp.asarray(cc).astype(BF16)
    sn = jnp.asarray(sn).astype(BF16)
    tab = pl.BlockSpec((C_CG, C_CG), lambda i: (0, 0))
    out = pl.BlockSpec((tm, D_MODEL), lambda i: (i, 0))
    return pl.pallas_call(
        _norm_chan_dft_kernel,
        grid=(m // tm,),
        in_specs=[
            pl.BlockSpec((tm, D_MODEL), lambda i: (i, 0)),
            pl.BlockSpec((1, D_MODEL), lambda i: (0, 0)),
            _mod_spec(sc_idx, tm, rows_per_mod, mod_off),
            _mod_spec(sh_idx, tm, rows_per_mod, mod_off),
            tab, tab,
        ],
        out_specs=[out, out],
        out_shape=[jax.ShapeDtypeStruct((m, D_MODEL), BF16)] * 2,
        scratch_shapes=[pltpu.VMEM((tm, D_MODEL), BF16)],
        compiler_params=_params("parallel"),
        name="norm_chan_dft",
    )(x, g.reshape(1, D_MODEL), mod4, mod4, cc, sn)


def _pos_dft_kernel(cn_ref, sn_ref, p_ref, q_ref, o_ref):
    o_ref[...] = (_dot(cn_ref[...], p_ref[...]) - _dot(sn_ref[...], q_ref[...])).astype(o_ref.dtype)


def _pos_dft(p3, q3, tn):
    bsz, n, _ = p3.shape
    cn, sn = _dft_tables(n)
    cn = jnp.asarray(cn).astype(BF16)
    sn = jnp.asarray(sn).astype(BF16)
    tab = pl.BlockSpec((n, n), lambda b, j: (0, 0))
    blk = pl.BlockSpec((None, n, tn), lambda b, j: (b, 0, j))
    return pl.pallas_call(
        _pos_dft_kernel,
        grid=(bsz, D_MODEL // tn),
        in_specs=[tab, tab, blk, blk],
        out_specs=blk,
        out_shape=jax.ShapeDtypeStruct((bsz, n, D_MODEL), BF16),
        compiler_params=_params("parallel", "parallel"),
        name="pos_dft",
    )(cn, sn, p3, q3)


FFN_SCHED = {name: i for i, name in enumerate((
    "up_rows", "up_mod", "up_col", "conv_col", "down_rows", "down_mod", "down_col",
    "norm_due", "reset_due", "out_due"))}


def _ffn_kernel(sched_ref, xc_ref, gf_ref, modu_ref, xp_ref, modd_ref, wgv_ref, cp_ref, wd_ref,
                mask_ref, o_ref, h_scr, acc_scr, u0_scr, u1_scr, act0_scr, act1_scr, *,
                final_norm, pieces):
    t = pl.program_id(0)
    tm, tf = act0_scr.shape
    lanes = mask_ref.shape[2]

    def due(name):
        return sched_ref[FFN_SCHED[name], t] != 0

    @pl.when(t == 0)
    def _():
        for ref in (u0_scr, u1_scr, act0_scr, act1_scr, acc_scr):
            ref[...] = jnp.zeros_like(ref)

    @pl.when(due("norm_due"))
    def _():
        _norm_mod_rows(xc_ref, gf_ref.at[0], modu_ref.at[1], modu_ref.at[0], h_scr)

    @pl.when(due("reset_due"))
    def _():
        acc_scr[...] = jnp.zeros_like(acc_scr)

    cb_rows = FFN_CONV_ROWS
    n_row_blocks = tm // cb_rows
    mask_rows = mask_ref.shape[1]

    def conv(u_old, k, r, cs):
        p0 = 4 * k
        ext = u_old[k, r * cb_rows:(r + 1) * cb_rows + 2 * HALO, cs]
        n_ext = cb_rows + 2 * HALO
        ms = slice((r * cb_rows) % mask_rows, (r * cb_rows) % mask_rows + cb_rows)
        inner = slice(HALO, HALO + cb_rows)
        prev = pltpu.roll(ext, 1, axis=0)[inner] * mask_ref[0, ms, :]
        nxt = pltpu.roll(ext, n_ext - 1, axis=0)[inner] * mask_ref[1, ms, :]
        return (prev * cp_ref[p0:p0 + 1, cs] + ext[inner] * cp_ref[p0 + 1:p0 + 2, cs]
                + nxt * cp_ref[p0 + 2:p0 + 3, cs] + cp_ref[p0 + 3:p0 + 4, cs])

    def conv_block(u_old, act_new, idx):
        c, r = divmod(idx, n_row_blocks)
        cs = slice(c * lanes, (c + 1) * lanes)
        gate = conv(u_old, 0, r, cs)
        val = conv(u_old, 1, r, cs)
        act_new[r * cb_rows:(r + 1) * cb_rows, cs] = (_silu(gate) * val).astype(BF16)

    def stages(u_new, u_old, act_new, act_old):
        n_blocks = (tf // lanes) * n_row_blocks
        up_pieces, down_pieces = pieces
        un = 2 * tf // up_pieces
        dn = D_MODEL // down_pieces
        per_half = tf // un
        h = h_scr[...]
        act = act_old[...]
        mxu = ([("up", q, un * D_MODEL) for q in range(up_pieces)]
               + [("down", q, dn * tf) for q in range(down_pieces)])
        early = n_blocks - FFN_TAIL_BLOCKS
        total = sum(w for _, _, w in mxu[:-1])
        done = 0
        spent = 0
        for n, (kind, q, w) in enumerate(mxu):
            if kind == "down":
                ns = slice(q * dn, (q + 1) * dn)
                acc_scr[:, ns] += _dot(act, wd_ref[:, ns])
            else:
                k, qq = divmod(q, per_half)
                ns = slice(qq * un, (qq + 1) * un)
                u_new[k, HALO:HALO + tm, ns] = _dot(h, wgv_ref[k, :, ns])
            spent += w
            target = n_blocks if n == len(mxu) - 1 else min(early, -(-early * spent // total))
            for idx in range(done, target):
                conv_block(u_old, act_new, idx)
            done = target

    @pl.when(t % 2 == 0)
    def _():
        stages(u0_scr, u1_scr, act1_scr, act0_scr)

    @pl.when(t % 2 == 1)
    def _():
        stages(u1_scr, u0_scr, act0_scr, act1_scr)

    @pl.when(due("out_due"))
    def _():
        rb = NORM_ROW_BLOCK

        def body(i, carry):
            rows = pl.ds(pl.multiple_of(i * rb, rb), rb)
            y = xp_ref[rows, :] + modd_ref[2] * acc_scr[rows, :]
            if final_norm:
                y = _rms(y, gf_ref[1])
            o_ref[rows, :] = y
            acc_scr[rows, :] = jnp.zeros((rb, D_MODEL), F32)
            return carry

        lax.fori_loop(0, tm // rb, body, 0, unroll=NORM_UNROLL)


def _ffn_conv_params(cw, cb, tf):
    cp = jnp.concatenate([cw[:, :D_FF], cb[None, :D_FF], cw[:, D_FF:], cb[None, D_FF:]], axis=0)
    return cp.reshape(8, D_FF // tf, tf).transpose(1, 0, 2)


def _conv_ffn(x, g, mod4, w_up, conv_p, w_down, fin, rows_per_mod, mod_off, row_len, tm,
              final_norm, pieces):
    m = x.shape[0]
    nf, _, tf = conv_p.shape
    n_tiles = (m // tm) * nf
    rb = max(row_len, FFN_CONV_ROWS)
    assert rb % row_len == 0 and rb % FFN_CONV_ROWS == 0 and tm % rb == 0
    pos = np.arange(rb) % row_len
    mask = np.stack([np.broadcast_to((pos != 0)[:, None], (rb, 128)),
                     np.broadcast_to((pos != row_len - 1)[:, None], (rb, 128))]).astype(np.float32)

    t = np.arange(n_tiles + 2)
    up = np.minimum(t, n_tiles - 1)
    cv = np.clip(t - 1, 0, n_tiles - 1)
    dn = np.clip(t - 2, 0, n_tiles - 1)
    sched = np.zeros((len(FFN_SCHED), n_tiles + 2), np.int32)
    sched[FFN_SCHED["up_rows"]] = up // nf
    sched[FFN_SCHED["up_mod"]] = mod_off + (up // nf * tm) // rows_per_mod
    sched[FFN_SCHED["up_col"]] = up % nf
    sched[FFN_SCHED["conv_col"]] = cv % nf
    sched[FFN_SCHED["down_rows"]] = dn // nf
    sched[FFN_SCHED["down_mod"]] = mod_off + (dn // nf * tm) // rows_per_mod
    sched[FFN_SCHED["down_col"]] = dn % nf
    sched[FFN_SCHED["norm_due"]] = (t < n_tiles) & (t % nf == 0)
    sched[FFN_SCHED["reset_due"]] = t == 2
    sched[FFN_SCHED["out_due"]] = (t >= 2) & ((t - 2) % nf == nf - 1)

    def row(name):
        return FFN_SCHED[name]

    def mod_spec(name):
        return pl.BlockSpec((None, 3, 1, D_MODEL), lambda t, s: (s[row(name), t], 1, 0, 0))

    grid_spec = pltpu.PrefetchScalarGridSpec(
        num_scalar_prefetch=1,
        grid=(n_tiles + 2,),
        in_specs=[
            pl.BlockSpec((tm, D_MODEL), lambda t, s: (s[row("up_rows"), t], 0)),
            pl.BlockSpec((2, 1, D_MODEL), lambda t, s: (0, 0, 0)),
            mod_spec("up_mod"),
            pl.BlockSpec((tm, D_MODEL), lambda t, s: (s[row("down_rows"), t], 0)),
            mod_spec("down_mod"),
            pl.BlockSpec((2, D_MODEL, tf), lambda t, s: (s[row("up_col"), t], 0, 0)),
            pl.BlockSpec((None, 8, tf), lambda t, s: (s[row("conv_col"), t], 0, 0)),
            pl.BlockSpec((tf, D_MODEL), lambda t, s: (s[row("down_col"), t], 0)),
            pl.BlockSpec((2, rb, 128), lambda t, s: (0, 0, 0)),
        ],
        out_specs=pl.BlockSpec((tm, D_MODEL), lambda t, s: (s[row("down_rows"), t], 0)),
        scratch_shapes=[pltpu.VMEM((tm, D_MODEL), BF16), pltpu.VMEM((tm, D_MODEL), F32),
                        pltpu.VMEM((2, tm + 2 * HALO, tf), F32),
                        pltpu.VMEM((2, tm + 2 * HALO, tf), F32),
                        pltpu.VMEM((tm, tf), BF16), pltpu.VMEM((tm, tf), BF16)],
    )
    return pl.pallas_call(
        functools.partial(_ffn_kernel, final_norm=final_norm, pieces=pieces),
        grid_spec=grid_spec,
        out_shape=jax.ShapeDtypeStruct((m, D_MODEL), F32),
        compiler_params=_params("arbitrary"),
        name="conv_ffn",
    )(jnp.asarray(sched), x, jnp.stack([g, fin]).reshape(2, 1, D_MODEL), mod4, x, mod4, w_up, conv_p,
      w_down, jnp.asarray(mask))


def kernel(x_prompt, x_sample, state_l0_hgrn, c, c_ctx, mod_w_0, mod_b_0, norm1_0, w_in_0, hgrn_lb, hgrn_gnorm_0, gmlp_vnorm_0, gmlp_ws_0, gmlp_bs_0, w_out_0, norm2_0, ffn_up_0, ffn_conv_w_0, ffn_conv_b_0, ffn_down_0, mod_w_1, mod_b_1, norm1_1, w_out_1, norm2_1, ffn_up_1, ffn_conv_w_1, ffn_conv_b_1, ffn_down_1, final_norm):
    n_dec = c.shape[0]
    cond = jnp.concatenate(
        [c, c_ctx[None, :], jnp.zeros((MOD_ROWS - n_dec - 1, D_MODEL), F32)], axis=0)
    mod4 = [_modulation(cond, w, b).reshape(MOD_ROWS, 6, 1, D_MODEL)
            for w, b in ((mod_w_0, mod_b_0), (mod_w_1, mod_b_1))]

    w_in = _cast_col_tiles(w_in_0, PROJ_COL_TILE)
    w_out0 = _cast_rows(w_out_0, CAST_ROWS)
    w_out1 = _cast_rows(w_out_1, CAST_ROWS)
    ffn = [(norm2_0, _cast_col_tiles(ffn_up_0, FFN_TF, pair_halves=True), _ffn_conv_params(ffn_conv_w_0, ffn_conv_b_0, FFN_TF),
            _cast_rows(ffn_down_0, CAST_ROWS)),
           (norm2_1, _cast_col_tiles(ffn_up_1, FFN_TF, pair_halves=True), _ffn_conv_params(ffn_conv_w_1, ffn_conv_b_1, FFN_TF),
            _cast_rows(ffn_down_1, CAST_ROWS))]

    def trunk(x3, s0, mod_off, per_batch_mod, row_len, emit_state):
        bsz, n, _ = x3.shape
        m = bsz * n
        rows_per_mod = n if per_batch_mod else m
        tm = ROW_TILE
        x = x3.reshape(m, D_MODEL)

        proj = _norm_proj(x, norm1_0, mod4[0], 1, 0, w_in, rows_per_mod, mod_off, PROJ_ROW_TILE, F32)
        res = _gla(proj.reshape(bsz, n, IN_WIDTH), hgrn_lb, hgrn_gnorm_0, s0,
                   emit_state)
        out_b = _gmlp(proj, gmlp_vnorm_0, gmlp_ws_0, gmlp_bs_0, GMLP_ROWS)
        x = _resid_proj(x, mod4[0], 2, [res[0].reshape(m, A_WIDTH), out_b], w_out0,
                        rows_per_mod, mod_off, tm)
        x = _conv_ffn(x, ffn[0][0], mod4[0], *ffn[0][1:], final_norm, rows_per_mod, mod_off,
                      row_len, tm, False, FFN_MXU_PIECES)

        p, q = _norm_chan_dft(x, norm1_1, mod4[1], 1, 0, rows_per_mod, mod_off, tm)
        four = _pos_dft(p.reshape(bsz, n, D_MODEL), q.reshape(bsz, n, D_MODEL), POS_DFT_COL_TILE)
        x = _resid_proj(x, mod4[1], 2, [four.reshape(m, D_MODEL)], w_out1, rows_per_mod, mod_off, tm)
        x = _conv_ffn(x, ffn[1][0], mod4[1], *ffn[1][1:], final_norm, rows_per_mod, mod_off,
                      row_len, tm, True, FFN_MXU_PIECES)
        return x.reshape(bsz, n, D_MODEL), (res[1] if emit_state else None)

    zero_state = jnp.zeros((x_prompt.shape[0], 2, A_HEADS, A_DK, A_DV), F32)
    y_prompt, state_new = trunk(x_prompt, zero_state, n_dec, False, x_prompt.shape[1], True)
    y_sample, _ = trunk(x_sample, state_l0_hgrn, 0, True, GRID_W, False)
    return (y_prompt, y_sample, state_new.astype(x_prompt.dtype))
```

```python
import functools

import numpy as np
import jax
import jax.numpy as jnp
from jax import lax
from jax.experimental import pallas as pl
from jax.experimental.pallas import tpu as pltpu

D_MODEL = 2048
A_WIDTH = 1024
A_HEADS = 8
A_DK = 128
A_DV = 128
B_WIDTH = 1024
B_GROUPS = 4
B_CHUNK = 128
B_CG = B_WIDTH // B_GROUPS
C_GROUPS = 4
C_CG = D_MODEL // C_GROUPS
SCAN_CHUNK = 64
D_FF = 5632
GRID_W = 64
IN_WIDTH = 5 * A_WIDTH + 2 * B_WIDTH
EPS = 1e-6

F32 = jnp.float32
BF16 = jnp.bfloat16

VMEM_LIMIT = 56 * 1024 * 1024
MOD_ROWS = 16
GLA_ROW_BLOCK = 256
GLA_HEADS_PER_STEP = 4
GLA_SCAN_UNROLL = 4
GLA_ROWS_PER_STEP = 2048
FFN_CONV_ROWS = 64
FFN_MXU_PIECES = (4, 8)
FFN_TAIL_BLOCKS = 1
HALO = 8
NORM_ROW_BLOCK = 16
NORM_UNROLL = 8
ROW_TILE = 512
FFN_TF = 512
PROJ_ROW_TILE = 1024
PROJ_COL_TILE = 1792
POS_DFT_COL_TILE = 1024
POS_DFT_ROWS = 1024
CHAN_DFT_ROW_TILE = 1024
CAST_ROWS = 512
GMLP_ROWS = 1024


def _params(*sem):
    return pltpu.CompilerParams(dimension_semantics=sem, vmem_limit_bytes=VMEM_LIMIT)


def _sigmoid(x):
    return 1.0 / (1.0 + jnp.exp(-x))


def _silu(x):
    return x * _sigmoid(x)


def _gelu_tanh(x):
    c = np.float32(np.sqrt(2.0 / np.pi))
    return 0.5 * x * (1.0 + jnp.tanh(c * (x + 0.044715 * (x * x * x))))


def _rms(x, g):
    return x * lax.rsqrt(jnp.mean(x * x, axis=-1, keepdims=True) + EPS) * g


def _norm_mod_rows(x_ref, g_ref, sc_ref, sh_ref, h_ref):
    rb = NORM_ROW_BLOCK

    def body(i, carry):
        rows = pl.ds(pl.multiple_of(i * rb, rb), rb)
        x = x_ref[rows, :]
        y = x * lax.rsqrt(jnp.mean(x * x, axis=-1, keepdims=True) + EPS)
        h_ref[rows, :] = (y * (g_ref[...] * (1.0 + sc_ref[...])) + sh_ref[...]).astype(h_ref.dtype)
        return carry

    lax.fori_loop(0, x_ref.shape[0] // rb, body, 0, unroll=NORM_UNROLL)


def _dot(a, b):
    return jnp.dot(a, b, preferred_element_type=F32)


def _dot_nt(a, b):
    return lax.dot_general(a, b, (((1,), (1,)), ((), ())), preferred_element_type=F32)


def _cast_kernel(w_ref, o_ref):
    o_ref[...] = w_ref[...].astype(o_ref.dtype)


def _cast_col_tiles(w, tn, pair_halves=False):
    k, n = w.shape
    half = n // tn // 2
    slot = (lambda j: jnp.where(j < half, 2 * j, 2 * (j - half) + 1)) if pair_halves else (lambda j: j)
    return pl.pallas_call(
        _cast_kernel,
        grid=(n // tn,),
        in_specs=[pl.BlockSpec((k, tn), lambda j: (0, j))],
        out_specs=pl.BlockSpec((None, k, tn), lambda j: (slot(j), 0, 0)),
        out_shape=jax.ShapeDtypeStruct((n // tn, k, tn), BF16),
        compiler_params=_params("parallel"),
        name="cast_col_tiles",
    )(w)


def _cast_rows(w, tk):
    k, n = w.shape
    return pl.pallas_call(
        _cast_kernel,
        grid=(k // tk,),
        in_specs=[pl.BlockSpec((tk, n), lambda i: (i, 0))],
        out_specs=pl.BlockSpec((tk, n), lambda i: (i, 0)),
        out_shape=jax.ShapeDtypeStruct((k, n), BF16),
        compiler_params=_params("parallel"),
        name="cast_rows",
    )(w)


def _mod_kernel(c_ref, w_ref, b_ref, o_ref):
    s = _silu(c_ref[...]).astype(BF16)
    o_ref[...] = _dot(s, w_ref[...].astype(BF16)) + b_ref[...]


def _modulation(cond, w, b):
    tn = 1024
    n = w.shape[1]
    return pl.pallas_call(
        _mod_kernel,
        grid=(n // tn,),
        in_specs=[
            pl.BlockSpec((MOD_ROWS, D_MODEL), lambda j: (0, 0)),
            pl.BlockSpec((D_MODEL, tn), lambda j: (0, j)),
            pl.BlockSpec((1, tn), lambda j: (0, j)),
        ],
        out_specs=pl.BlockSpec((MOD_ROWS, tn), lambda j: (0, j)),
        out_shape=jax.ShapeDtypeStruct((MOD_ROWS, n), F32),
        compiler_params=_params("arbitrary"),
        name="modulation",
    )(cond, w, b.reshape(1, n))


def _mod_spec(which, tm, rows_per_mod, mod_off):
    return pl.BlockSpec(
        (None, None, 1, D_MODEL),
        lambda i, *_: (mod_off + (i * tm) // rows_per_mod, which, 0, 0))


def _norm_proj_kernel(x_ref, g_ref, sc_ref, sh_ref, w_ref, o_ref, h_scr):
    @pl.when(pl.program_id(1) == 0)
    def _():
        _norm_mod_rows(x_ref, g_ref, sc_ref, sh_ref, h_scr)

    o_ref[...] = _dot(h_scr[...], w_ref[...]).astype(o_ref.dtype)


def _norm_proj(x, g, mod4, sc_idx, sh_idx, w_tiles, rows_per_mod, mod_off, tm, out_dtype):
    m = x.shape[0]
    nt, _, tn = w_tiles.shape
    n = nt * tn
    return pl.pallas_call(
        _norm_proj_kernel,
        grid=(m // tm, n // tn),
        in_specs=[
            pl.BlockSpec((tm, D_MODEL), lambda i, j: (i, 0)),
            pl.BlockSpec((1, D_MODEL), lambda i, j: (0, 0)),
            _mod_spec(sc_idx, tm, rows_per_mod, mod_off),
            _mod_spec(sh_idx, tm, rows_per_mod, mod_off),
            pl.BlockSpec((None, D_MODEL, tn), lambda i, j: (j, 0, 0)),
        ],
        out_specs=pl.BlockSpec((tm, tn), lambda i, j: (i, j)),
        out_shape=jax.ShapeDtypeStruct((m, n), out_dtype),
        scratch_shapes=[pltpu.VMEM((tm, D_MODEL), BF16)],
        compiler_params=_params("parallel", "arbitrary"),
        name="norm_proj",
    )(x, g.reshape(1, D_MODEL), mod4, mod4, w_tiles)


def _gla_head(qa_ref, fzf_ref, fzb_ref, ia_ref, ga_ref, lbp_ref, gn_ref, s0_ref,
              out_ref, *rest, n, emit_state):
    if emit_state:
        snew_ref, ops_scr, dec_scr, p_scr, kv_scr, o_scr, st_scr = rest
    else:
        ops_scr, dec_scr, p_scr, kv_scr, o_scr, st_scr = rest
    c_len = SCAN_CHUNK
    nc = n // c_len
    mid = c_len // 2
    rb = GLA_ROW_BLOCK
    cpb = rb // c_len

    def lower_bound(d):
        a = [lbp_ref[d, l] for l in range(lbp_ref.shape[1])]
        mx = functools.reduce(jnp.maximum, a)
        e = [jnp.exp(t - mx) for t in a]
        return e[0] / functools.reduce(lambda u, w: u + w, e)

    lbs = (lower_bound(0), lower_bound(1))
    pos = lax.broadcasted_iota(jnp.int32, (rb, A_DK), 0) % c_len

    def operands(i):
        r0 = pl.multiple_of(i * rb, rb)
        rows = pl.ds(r0, rb)
        q = _silu(qa_ref[rows, :]) * (A_DK ** -0.5)
        for d, fz_ref in enumerate((fzf_ref, fzb_ref)):
            lb = lbs[d]
            f = lb + (1.0 - lb) * _sigmoid(fz_ref[rows, :])
            k = 1.0 - f
            b = jnp.log2(f)
            for s in (1, 2, 4, 8, 16, 32):
                if d == 0:
                    b = b + jnp.where(pos >= s, pltpu.roll(b, s, axis=0), 0.0)
                else:
                    b = b + jnp.where(pos < c_len - s, pltpu.roll(b, rb - s, axis=0), 0.0)
            for ci in range(cpb):
                sl = slice(ci * c_len, (ci + 1) * c_len)
                bc = b[sl]
                if d == 0:
                    ref, b_last = bc[mid - 1:mid], bc[c_len - 1:c_len]
                else:
                    ref, b_last = bc[c_len - mid:c_len - mid + 1], bc[0:1]
                qe = q[sl] * jnp.exp2(bc - ref)
                ke = k[sl] * jnp.exp2(ref - bc)
                crow = pl.ds(r0 + ci * c_len, c_len)
                ops_scr[d, 0, crow, :] = qe.astype(BF16)
                ops_scr[d, 1, crow, :] = ke.astype(BF16)
                ops_scr[d, 2, crow, :] = (qe * jnp.exp2(ref)).astype(BF16)
                ops_scr[d, 3, crow, :] = (ke * jnp.exp2(b_last - ref)).astype(BF16)
                dec_scr[d, i * cpb + ci] = jnp.broadcast_to(jnp.exp2(b_last), (8, A_DK))

    ri = lax.broadcasted_iota(jnp.int32, (c_len, c_len), 0)
    ci_ = lax.broadcasted_iota(jnp.int32, (c_len, c_len), 1)
    keep = (ri >= ci_, ri <= ci_)

    def chunk_products(c):
        rows = pl.ds(pl.multiple_of(c * c_len, c_len), c_len)
        vt = ia_ref[rows, :].T.astype(BF16)
        for d in range(2):
            scores = jnp.where(keep[d], _dot_nt(ops_scr[d, 0, rows, :], ops_scr[d, 1, rows, :]), 0.0)
            p_scr[d, rows, :] = scores.astype(BF16)
            kv_scr[d, c] = _dot(vt, ops_scr[d, 3, rows, :])

    def init_state():
        for d in range(2):
            st_scr[d] = s0_ref[d].T

    def scan_step(c):
        for d in range(2):
            cidx = c if d == 0 else nc - 1 - c
            rows = pl.ds(pl.multiple_of(cidx * c_len, c_len), c_len)
            st = st_scr[d]
            o_scr[d, rows, :] = (_dot(p_scr[d, rows, :], ia_ref[rows, :].astype(BF16))
                                 + _dot_nt(ops_scr[d, 2, rows, :], st.astype(BF16)))
            st_scr[d] = dec_scr[d, cidx][0:1, :] * st + kv_scr[d, cidx]

    def write_state():
        if emit_state:
            for d in range(2):
                snew_ref[d] = st_scr[d].T

    def finish(i):
        rows = pl.ds(pl.multiple_of(i * rb, rb), rb)
        o = _rms(o_scr[0, rows, :] + o_scr[1, rows, :], gn_ref[...]) * _silu(ga_ref[rows, :])
        out_ref[rows, :] = o.astype(out_ref.dtype)

    return operands, chunk_products, init_state, scan_step, write_state, finish


def _gla_kernel(qa_ref, fzf_ref, fzb_ref, ia_ref, ga_ref, lbp_ref, gn_ref, s0_ref, out_ref, *rest,
                n, emit_state, heads):
    snew_ref = rest[0] if emit_state else None
    scratch = rest[1:] if emit_state else rest
    phases = []
    for hh in range(heads):
        cols = pl.ds(hh * A_DK, A_DK)
        head_rest = tuple(r.at[hh] for r in scratch)
        if emit_state:
            head_rest = (snew_ref.at[:, hh],) + head_rest
        phases.append(_gla_head(
            qa_ref.at[:, cols], fzf_ref.at[:, cols], fzb_ref.at[:, cols], ia_ref.at[:, cols],
            ga_ref.at[:, cols], lbp_ref.at[:, :, hh], gn_ref, s0_ref.at[:, hh],
            out_ref.at[:, cols], *head_rest, n=n, emit_state=emit_state))
    operands, chunk_products, init_state, scan_step, write_state, finish = zip(*phases)
    nc = n // SCAN_CHUNK
    unroll = True if nc <= 4 else GLA_SCAN_UNROLL

    def each(fns):
        def body(i, carry):
            for fn in fns:
                fn(i)
            return carry
        return body

    lax.fori_loop(0, n // GLA_ROW_BLOCK, each(operands), 0)
    lax.fori_loop(0, nc, each(chunk_products), 0, unroll=unroll)
    for fn in init_state:
        fn()
    lax.fori_loop(0, nc, each(scan_step), 0, unroll=unroll)
    for fn in write_state:
        fn()
    lax.fori_loop(0, n // GLA_ROW_BLOCK, each(finish), 0)


def _gla(proj3, hgrn_lb, gnorm, s0, emit_state):
    bsz, n, _ = proj3.shape
    h = A_HEADS
    g = max(GLA_HEADS_PER_STEP, min(h, GLA_ROWS_PER_STEP // n))
    hg = h // g

    def col(k):
        return pl.BlockSpec((None, n, g * A_DK), lambda b, hh: (b, 0, k * hg + hh))

    n_lb = hgrn_lb.shape[1]
    lb_spec = pl.BlockSpec((2, n_lb, g, 1, A_DK), lambda b, hh: (0, 0, hh, 0, 0))
    st_spec = pl.BlockSpec((None, 2, g, A_DK, A_DV), lambda b, hh: (b, 0, hh, 0, 0))
    out_shape = [jax.ShapeDtypeStruct((bsz, n, A_WIDTH), BF16)]
    out_specs = [pl.BlockSpec((None, n, g * A_DV), lambda b, hh: (b, 0, hh))]
    if emit_state:
        out_shape.append(jax.ShapeDtypeStruct((bsz, 2, h, A_DK, A_DV), F32))
        out_specs.append(st_spec)
    res = pl.pallas_call(
        functools.partial(_gla_kernel, n=n, emit_state=emit_state, heads=g),
        grid=(bsz, hg),
        in_specs=[col(0), col(1), col(2), col(3), col(4), lb_spec,
                  pl.BlockSpec((1, A_DV), lambda b, hh: (0, 0)), st_spec],
        out_specs=out_specs,
        out_shape=out_shape,
        scratch_shapes=[pltpu.VMEM((g, 2, 4, n, A_DK), BF16),
                        pltpu.VMEM((g, 2, n // SCAN_CHUNK, 8, A_DK), F32),
                        pltpu.VMEM((g, 2, n, SCAN_CHUNK), BF16),
                        pltpu.VMEM((g, 2, n // SCAN_CHUNK, A_DV, A_DK), F32),
                        pltpu.VMEM((g, 2, n, A_DV), F32),
                        pltpu.VMEM((g, 2, A_DV, A_DK), F32)],
        compiler_params=_params("parallel", "parallel"),
        name="hgrn2",
    )(proj3, proj3, proj3, proj3, proj3, hgrn_lb.reshape(2, n_lb, h, 1, A_DK),
      gnorm.reshape(1, A_DV), s0)
    return res


def _gmlp_kernel(ub_ref, vb_ref, vn_ref, ws_ref, bst_ref, out_ref, *, rows):
    for g in range(B_GROUPS):
        cs = slice(g * B_CG, (g + 1) * B_CG)
        vv = _rms(_gelu_tanh(vb_ref[:, cs]), vn_ref[:, cs]).astype(BF16)
        w = ws_ref[g].astype(BF16)
        bias = bst_ref[:, g:g + 1]
        for c in range(rows // B_CHUNK):
            rs = slice(c * B_CHUNK, (c + 1) * B_CHUNK)
            mixed = _dot(w, vv[rs, :]) + bias
            out_ref[rs, cs] = (_gelu_tanh(ub_ref[rs, cs]) * mixed).astype(out_ref.dtype)


def _gmlp(proj, vnorm, ws, bs, rows):
    m = proj.shape[0]
    ub_blk = 5 * A_WIDTH // B_WIDTH
    return pl.pallas_call(
        functools.partial(_gmlp_kernel, rows=rows),
        grid=(m // rows,),
        in_specs=[
            pl.BlockSpec((rows, B_WIDTH), lambda i: (i, ub_blk)),
            pl.BlockSpec((rows, B_WIDTH), lambda i: (i, ub_blk + 1)),
            pl.BlockSpec((1, B_WIDTH), lambda i: (0, 0)),
            pl.BlockSpec((B_GROUPS, B_CHUNK, B_CHUNK), lambda i: (0, 0, 0)),
            pl.BlockSpec((B_CHUNK, B_GROUPS), lambda i: (0, 0)),
        ],
        out_specs=pl.BlockSpec((rows, B_WIDTH), lambda i: (i, 0)),
        out_shape=jax.ShapeDtypeStruct((m, B_WIDTH), BF16),
        compiler_params=_params("parallel"),
        name="gmlp",
    )(proj, proj, vnorm.reshape(1, B_WIDTH), ws, bs.T)


def _resid_proj_kernel(*refs, n_parts):
    x_ref, gate_ref = refs[0], refs[1]
    a_refs = refs[2:2 + n_parts]
    w_refs = refs[2 + n_parts:2 + 2 * n_parts]
    o_ref = refs[2 + 2 * n_parts]
    acc = _dot(a_refs[0][...], w_refs[0][...])
    for a_ref, w_ref in zip(a_refs[1:], w_refs[1:]):
        acc = acc + _dot(a_ref[...], w_ref[...])
    o_ref[...] = x_ref[...] + gate_ref[...] * acc


def _resid_proj(x, mod4, gate_idx, parts, w, rows_per_mod, mod_off, tm):
    m = x.shape[0]
    n_parts = len(parts)
    kp = parts[0].shape[1]
    in_specs = [pl.BlockSpec((tm, D_MODEL), lambda i: (i, 0)),
                _mod_spec(gate_idx, tm, rows_per_mod, mod_off)]
    in_specs += [pl.BlockSpec((tm, kp), lambda i: (i, 0)) for _ in parts]
    in_specs += [pl.BlockSpec((kp, D_MODEL), lambda i, p=p: (p, 0)) for p in range(n_parts)]
    return pl.pallas_call(
        functools.partial(_resid_proj_kernel, n_parts=n_parts),
        grid=(m // tm,),
        in_specs=in_specs,
        out_specs=pl.BlockSpec((tm, D_MODEL), lambda i: (i, 0)),
        out_shape=jax.ShapeDtypeStruct((m, D_MODEL), F32),
        compiler_params=_params("parallel"),
        name="resid_proj",
    )(x, mod4, *parts, *([w] * n_parts))


def _dft_tables(n):
    idx = np.arange(n, dtype=np.int64)
    ang = 2.0 * np.pi * ((idx[:, None] * idx[None, :]) % n).astype(np.float64) / n
    s = 1.0 / np.sqrt(n)
    return (np.cos(ang) * s).astype(np.float32), (np.sin(ang) * s).astype(np.float32)


def _norm_chan_dft_kernel(x_ref, g_ref, sc_ref, sh_ref, cc_ref, sn_ref, p_ref, q_ref, h_scr):
    _norm_mod_rows(x_ref, g_ref, sc_ref, sh_ref, h_scr)
    for g in range(C_GROUPS):
        cs = slice(g * C_CG, (g + 1) * C_CG)
        p_ref[:, cs] = _dot(h_scr[:, cs], cc_ref[...]).astype(p_ref.dtype)
        q_ref[:, cs] = _dot(h_scr[:, cs], sn_ref[...]).astype(q_ref.dtype)


def _norm_chan_dft(x, g, mod4, sc_idx, sh_idx, rows_per_mod, mod_off, tm):
    m = x.shape[0]
    cc, sn = _dft_tables(C_CG)
    cc = jnp.asarray(cc).astype(BF16)
    sn = jnp.asarray(sn).astype(BF16)
    tab = pl.BlockSpec((C_CG, C_CG), lambda i: (0, 0))
    out = pl.BlockSpec((tm, D_MODEL), lambda i: (i, 0))
    return pl.pallas_call(
        _norm_chan_dft_kernel,
        grid=(m // tm,),
        in_specs=[
            pl.BlockSpec((tm, D_MODEL), lambda i: (i, 0)),
            pl.BlockSpec((1, D_MODEL), lambda i: (0, 0)),
            _mod_spec(sc_idx, tm, rows_per_mod, mod_off),
            _mod_spec(sh_idx, tm, rows_per_mod, mod_off),
            tab, tab,
        ],
        out_specs=[out, out],
        out_shape=[jax.ShapeDtypeStruct((m, D_MODEL), BF16)] * 2,
        scratch_shapes=[pltpu.VMEM((tm, D_MODEL), BF16)],
        compiler_params=_params("parallel"),
        name="norm_chan_dft",
    )(x, g.reshape(1, D_MODEL), mod4, mod4, cc, sn)


def _pos_dft_kernel(cn_ref, sn_ref, p_ref, q_ref, o_ref):
    for b in range(o_ref.shape[0]):
        o_ref[b] = (_dot(cn_ref[...], p_ref[b]) - _dot(sn_ref[...], q_ref[b])).astype(o_ref.dtype)


def _pos_dft(p3, q3, tn):
    bsz, n, _ = p3.shape
    bb = max(1, min(bsz, POS_DFT_ROWS // n))
    cn, sn = _dft_tables(n)
    cn = jnp.asarray(cn).astype(BF16)
    sn = jnp.asarray(sn).astype(BF16)
    tab = pl.BlockSpec((n, n), lambda b, j: (0, 0))
    blk = pl.BlockSpec((bb, n, tn), lambda b, j: (b, 0, j))
    return pl.pallas_call(
        _pos_dft_kernel,
        grid=(bsz // bb, D_MODEL // tn),
        in_specs=[tab, tab, blk, blk],
        out_specs=blk,
        out_shape=jax.ShapeDtypeStruct((bsz, n, D_MODEL), BF16),
        compiler_params=_params("parallel", "parallel"),
        name="pos_dft",
    )(cn, sn, p3, q3)


FFN_SCHED = {name: i for i, name in enumerate((
    "up_rows", "up_mod", "up_col", "conv_col", "down_rows", "down_mod", "down_col",
    "norm_due", "reset_due", "out_due"))}


def _ffn_kernel(sched_ref, xc_ref, gf_ref, modu_ref, xp_ref, modd_ref, wgv_ref, cp_ref, wd_ref,
                mask_ref, o_ref, h_scr, acc_scr, u0_scr, u1_scr, act0_scr, act1_scr, *,
                final_norm, pieces):
    t = pl.program_id(0)
    tm, tf = act0_scr.shape
    lanes = mask_ref.shape[2]

    def due(name):
        return sched_ref[FFN_SCHED[name], t] != 0

    @pl.when(t == 0)
    def _():
        for ref in (u0_scr, u1_scr, act0_scr, act1_scr, acc_scr):
            ref[...] = jnp.zeros_like(ref)

    @pl.when(due("norm_due"))
    def _():
        _norm_mod_rows(xc_ref, gf_ref.at[0], modu_ref.at[1], modu_ref.at[0], h_scr)

    @pl.when(due("reset_due"))
    def _():
        acc_scr[...] = jnp.zeros_like(acc_scr)

    cb_rows = FFN_CONV_ROWS
    n_row_blocks = tm // cb_rows
    mask_rows = mask_ref.shape[1]

    def conv(u_old, k, r, cs):
        p0 = 4 * k
        ext = u_old[k, r * cb_rows:(r + 1) * cb_rows + 2 * HALO, cs]
        n_ext = cb_rows + 2 * HALO
        ms = slice((r * cb_rows) % mask_rows, (r * cb_rows) % mask_rows + cb_rows)
        inner = slice(HALO, HALO + cb_rows)
        prev = pltpu.roll(ext, 1, axis=0)[inner] * mask_ref[0, ms, :]
        nxt = pltpu.roll(ext, n_ext - 1, axis=0)[inner] * mask_ref[1, ms, :]
        return (prev * cp_ref[p0:p0 + 1, cs] + ext[inner] * cp_ref[p0 + 1:p0 + 2, cs]
                + nxt * cp_ref[p0 + 2:p0 + 3, cs] + cp_ref[p0 + 3:p0 + 4, cs])

    def conv_block(u_old, act_new, idx):
        c, r = divmod(idx, n_row_blocks)
        cs = slice(c * lanes, (c + 1) * lanes)
        gate = conv(u_old, 0, r, cs)
        val = conv(u_old, 1, r, cs)
        act_new[r * cb_rows:(r + 1) * cb_rows, cs] = (_silu(gate) * val).astype(BF16)

    def stages(u_new, u_old, act_new, act_old):
        n_blocks = (tf // lanes) * n_row_blocks
        up_pieces, down_pieces = pieces
        un = 2 * tf // up_pieces
        dn = D_MODEL // down_pieces
        per_half = tf // un
        h = h_scr[...]
        act = act_old[...]
        mxu = ([("up", q, un * D_MODEL) for q in range(up_pieces)]
               + [("down", q, dn * tf) for q in range(down_pieces)])
        early = n_blocks - FFN_TAIL_BLOCKS
        total = sum(w for _, _, w in mxu[:-1])
        done = 0
        spent = 0
        for n, (kind, q, w) in enumerate(mxu):
            if kind == "down":
                ns = slice(q * dn, (q + 1) * dn)
                acc_scr[:, ns] += _dot(act, wd_ref[:, ns])
            else:
                k, qq = divmod(q, per_half)
                ns = slice(qq * un, (qq + 1) * un)
                u_new[k, HALO:HALO + tm, ns] = _dot(h, wgv_ref[k, :, ns])
            spent += w
            target = n_blocks if n == len(mxu) - 1 else min(early, -(-early * spent // total))
            for idx in range(done, target):
                conv_block(u_old, act_new, idx)
            done = target

    @pl.when(t % 2 == 0)
    def _():
        stages(u0_scr, u1_scr, act1_scr, act0_scr)

    @pl.when(t % 2 == 1)
    def _():
        stages(u1_scr, u0_scr, act0_scr, act1_scr)

    @pl.when(due("out_due"))
    def _():
        rb = NORM_ROW_BLOCK

        def body(i, carry):
            rows = pl.ds(pl.multiple_of(i * rb, rb), rb)
            y = xp_ref[rows, :] + modd_ref[2] * acc_scr[rows, :]
            if final_norm:
                y = _rms(y, gf_ref[1])
            o_ref[rows, :] = y
            acc_scr[rows, :] = jnp.zeros((rb, D_MODEL), F32)
            return carry

        lax.fori_loop(0, tm // rb, body, 0, unroll=NORM_UNROLL)


def _ffn_conv_params(cw, cb, tf):
    cp = jnp.concatenate([cw[:, :D_FF], cb[None, :D_FF], cw[:, D_FF:], cb[None, D_FF:]], axis=0)
    return cp.reshape(8, D_FF // tf, tf).transpose(1, 0, 2)


def _conv_ffn(x, g, mod4, w_up, conv_p, w_down, fin, rows_per_mod, mod_off, row_len, tm,
              final_norm, pieces):
    m = x.shape[0]
    nf, _, tf = conv_p.shape
    n_tiles = (m // tm) * nf
    rb = max(row_len, FFN_CONV_ROWS)
    assert rb % row_len == 0 and rb % FFN_CONV_ROWS == 0 and tm % rb == 0
    pos = np.arange(rb) % row_len
    mask = np.stack([np.broadcast_to((pos != 0)[:, None], (rb, 128)),
                     np.broadcast_to((pos != row_len - 1)[:, None], (rb, 128))]).astype(np.float32)

    t = np.arange(n_tiles + 2)
    up = np.minimum(t, n_tiles - 1)
    cv = np.clip(t - 1, 0, n_tiles - 1)
    dn = np.clip(t - 2, 0, n_tiles - 1)
    sched = np.zeros((len(FFN_SCHED), n_tiles + 2), np.int32)
    sched[FFN_SCHED["up_rows"]] = up // nf
    sched[FFN_SCHED["up_mod"]] = mod_off + (up // nf * tm) // rows_per_mod
    sched[FFN_SCHED["up_col"]] = up % nf
    sched[FFN_SCHED["conv_col"]] = cv % nf
    sched[FFN_SCHED["down_rows"]] = dn // nf
    sched[FFN_SCHED["down_mod"]] = mod_off + (dn // nf * tm) // rows_per_mod
    sched[FFN_SCHED["down_col"]] = dn % nf
    sched[FFN_SCHED["norm_due"]] = (t < n_tiles) & (t % nf == 0)
    sched[FFN_SCHED["reset_due"]] = t == 2
    sched[FFN_SCHED["out_due"]] = (t >= 2) & ((t - 2) % nf == nf - 1)

    def row(name):
        return FFN_SCHED[name]

    def mod_spec(name):
        return pl.BlockSpec((None, 3, 1, D_MODEL), lambda t, s: (s[row(name), t], 1, 0, 0))

    grid_spec = pltpu.PrefetchScalarGridSpec(
        num_scalar_prefetch=1,
        grid=(n_tiles + 2,),
        in_specs=[
            pl.BlockSpec((tm, D_MODEL), lambda t, s: (s[row("up_rows"), t], 0)),
            pl.BlockSpec((2, 1, D_MODEL), lambda t, s: (0, 0, 0)),
            mod_spec("up_mod"),
            pl.BlockSpec((tm, D_MODEL), lambda t, s: (s[row("down_rows"), t], 0)),
            mod_spec("down_mod"),
            pl.BlockSpec((2, D_MODEL, tf), lambda t, s: (s[row("up_col"), t], 0, 0)),
            pl.BlockSpec((None, 8, tf), lambda t, s: (s[row("conv_col"), t], 0, 0)),
            pl.BlockSpec((tf, D_MODEL), lambda t, s: (s[row("down_col"), t], 0)),
            pl.BlockSpec((2, rb, 128), lambda t, s: (0, 0, 0)),
        ],
        out_specs=pl.BlockSpec((tm, D_MODEL), lambda t, s: (s[row("down_rows"), t], 0)),
        scratch_shapes=[pltpu.VMEM((tm, D_MODEL), BF16), pltpu.VMEM((tm, D_MODEL), F32),
                        pltpu.VMEM((2, tm + 2 * HALO, tf), F32),
                        pltpu.VMEM((2, tm + 2 * HALO, tf), F32),
                        pltpu.VMEM((tm, tf), BF16), pltpu.VMEM((tm, tf), BF16)],
    )
    return pl.pallas_call(
        functools.partial(_ffn_kernel, final_norm=final_norm, pieces=pieces),
        grid_spec=grid_spec,
        out_shape=jax.ShapeDtypeStruct((m, D_MODEL), F32),
        compiler_params=_params("arbitrary"),
        name="conv_ffn",
    )(jnp.asarray(sched), x, jnp.stack([g, fin]).reshape(2, 1, D_MODEL), mod4, x, mod4, w_up, conv_p,
      w_down, jnp.asarray(mask))


def kernel(x_prompt, x_sample, state_l0_hgrn, c, c_ctx, mod_w_0, mod_b_0, norm1_0, w_in_0, hgrn_lb, hgrn_gnorm_0, gmlp_vnorm_0, gmlp_ws_0, gmlp_bs_0, w_out_0, norm2_0, ffn_up_0, ffn_conv_w_0, ffn_conv_b_0, ffn_down_0, mod_w_1, mod_b_1, norm1_1, w_out_1, norm2_1, ffn_up_1, ffn_conv_w_1, ffn_conv_b_1, ffn_down_1, final_norm):
    n_dec = c.shape[0]
    cond = jnp.concatenate(
        [c, c_ctx[None, :], jnp.zeros((MOD_ROWS - n_dec - 1, D_MODEL), F32)], axis=0)
    mod4 = [_modulation(cond, w, b).reshape(MOD_ROWS, 6, 1, D_MODEL)
            for w, b in ((mod_w_0, mod_b_0), (mod_w_1, mod_b_1))]

    w_in = _cast_col_tiles(w_in_0, PROJ_COL_TILE)
    w_out0 = _cast_rows(w_out_0, CAST_ROWS)
    w_out1 = _cast_rows(w_out_1, CAST_ROWS)
    ffn = [(norm2_0, _cast_col_tiles(ffn_up_0, FFN_TF, pair_halves=True), _ffn_conv_params(ffn_conv_w_0, ffn_conv_b_0, FFN_TF),
            _cast_rows(ffn_down_0, CAST_ROWS)),
           (norm2_1, _cast_col_tiles(ffn_up_1, FFN_TF, pair_halves=True), _ffn_conv_params(ffn_conv_w_1, ffn_conv_b_1, FFN_TF),
            _cast_rows(ffn_down_1, CAST_ROWS))]

    def trunk(x3, s0, mod_off, per_batch_mod, row_len, emit_state):
        bsz, n, _ = x3.shape
        m = bsz * n
        rows_per_mod = n if per_batch_mod else m
        tm = ROW_TILE
        x = x3.reshape(m, D_MODEL)

        proj = _norm_proj(x, norm1_0, mod4[0], 1, 0, w_in, rows_per_mod, mod_off, PROJ_ROW_TILE, F32)
        res = _gla(proj.reshape(bsz, n, IN_WIDTH), hgrn_lb, hgrn_gnorm_0, s0,
                   emit_state)
        out_b = _gmlp(proj, gmlp_vnorm_0, gmlp_ws_0, gmlp_bs_0, GMLP_ROWS)
        x = _resid_proj(x, mod4[0], 2, [res[0].reshape(m, A_WIDTH), out_b], w_out0,
                        rows_per_mod, mod_off, tm)
        x = _conv_ffn(x, ffn[0][0], mod4[0], *ffn[0][1:], final_norm, rows_per_mod, mod_off,
                      row_len, tm, False, FFN_MXU_PIECES)

        p, q = _norm_chan_dft(x, norm1_1, mod4[1], 1, 0, rows_per_mod, mod_off, CHAN_DFT_ROW_TILE)
        four = _pos_dft(p.reshape(bsz, n, D_MODEL), q.reshape(bsz, n, D_MODEL), POS_DFT_COL_TILE)
        x = _resid_proj(x, mod4[1], 2, [four.reshape(m, D_MODEL)], w_out1, rows_per_mod, mod_off, tm)
        x = _conv_ffn(x, ffn[1][0], mod4[1], *ffn[1][1:], final_norm, rows_per_mod, mod_off,
                      row_len, tm, True, FFN_MXU_PIECES)
        return x.reshape(bsz, n, D_MODEL), (res[1] if emit_state else None)

    zero_state = jnp.zeros((x_prompt.shape[0], 2, A_HEADS, A_DK, A_DV), F32)
    y_prompt, state_new = trunk(x_prompt, zero_state, n_dec, False, x_prompt.shape[1], True)
    y_sample, _ = trunk(x_sample, state_l0_hgrn, 0, True, GRID_W, False)
    return (y_prompt, y_sample, state_new.astype(x_prompt.dtype))
```

```python
import functools

import numpy as np
import jax
import jax.numpy as jnp
from jax import lax
from jax.experimental import pallas as pl
from jax.experimental.pallas import tpu as pltpu

D_MODEL = 2048
A_WIDTH = 1024
A_HEADS = 8
A_DK = 128
A_DV = 128
B_WIDTH = 1024
B_GROUPS = 4
B_CHUNK = 128
B_CG = B_WIDTH // B_GROUPS
C_GROUPS = 4
C_CG = D_MODEL // C_GROUPS
SCAN_CHUNK = 64
D_FF = 5632
GRID_W = 64
IN_WIDTH = 5 * A_WIDTH + 2 * B_WIDTH
EPS = 1e-6

F32 = jnp.float32
BF16 = jnp.bfloat16

VMEM_LIMIT = 56 * 1024 * 1024
MOD_ROWS = 16
GLA_ROW_BLOCK = 256
GLA_HEADS_PER_STEP = 4
GLA_SCAN_UNROLL = 4
GLA_ROWS_PER_STEP = 2048
FFN_CONV_ROWS = 64
FFN_MXU_PIECES = (4, 8)
FFN_TAIL_BLOCKS = 1
HALO = 8
NORM_ROW_BLOCK = 16
NORM_UNROLL = 8
ROW_TILE = 512
FFN_TF = 512
PROJ_ROW_TILE = 1024
PROJ_COL_TILE = 1792
POS_DFT_COL_TILE = 1024
POS_DFT_ROWS = 1024
CHAN_DFT_ROW_TILE = 512
CAST_ROWS = 512
GMLP_ROWS = 1024


def _params(*sem):
    return pltpu.CompilerParams(dimension_semantics=sem, vmem_limit_bytes=VMEM_LIMIT)


def _sigmoid(x):
    return 1.0 / (1.0 + jnp.exp(-x))


def _silu(x):
    return x * _sigmoid(x)


def _gelu_tanh(x):
    c = np.float32(np.sqrt(2.0 / np.pi))
    return 0.5 * x * (1.0 + jnp.tanh(c * (x + 0.044715 * (x * x * x))))


def _rms(x, g):
    return x * lax.rsqrt(jnp.mean(x * x, axis=-1, keepdims=True) + EPS) * g


def _norm_mod_rows(x_ref, g_ref, sc_ref, sh_ref, h_ref):
    rb = NORM_ROW_BLOCK

    def body(i, carry):
        rows = pl.ds(pl.multiple_of(i * rb, rb), rb)
        x = x_ref[rows, :]
        y = x * lax.rsqrt(jnp.mean(x * x, axis=-1, keepdims=True) + EPS)
        h_ref[rows, :] = (y * (g_ref[...] * (1.0 + sc_ref[...])) + sh_ref[...]).astype(h_ref.dtype)
        return carry

    lax.fori_loop(0, x_ref.shape[0] // rb, body, 0, unroll=NORM_UNROLL)


def _dot(a, b):
    return jnp.dot(a, b, preferred_element_type=F32)


def _dot_nt(a, b):
    return lax.dot_general(a, b, (((1,), (1,)), ((), ())), preferred_element_type=F32)


def _cast_kernel(w_ref, o_ref):
    o_ref[...] = w_ref[...].astype(o_ref.dtype)


def _cast_col_tiles(w, tn, pair_halves=False):
    k, n = w.shape
    half = n // tn // 2
    slot = (lambda j: jnp.where(j < half, 2 * j, 2 * (j - half) + 1)) if pair_halves else (lambda j: j)
    return pl.pallas_call(
        _cast_kernel,
        grid=(n // tn,),
        in_specs=[pl.BlockSpec((k, tn), lambda j: (0, j))],
        out_specs=pl.BlockSpec((None, k, tn), lambda j: (slot(j), 0, 0)),
        out_shape=jax.ShapeDtypeStruct((n // tn, k, tn), BF16),
        compiler_params=_params("parallel"),
        name="cast_col_tiles",
    )(w)


def _cast_rows(w, tk):
    k, n = w.shape
    return pl.pallas_call(
        _cast_kernel,
        grid=(k // tk,),
        in_specs=[pl.BlockSpec((tk, n), lambda i: (i, 0))],
        out_specs=pl.BlockSpec((tk, n), lambda i: (i, 0)),
        out_shape=jax.ShapeDtypeStruct((k, n), BF16),
        compiler_params=_params("parallel"),
        name="cast_rows",
    )(w)


def _mod_kernel(c_ref, w_ref, b_ref, o_ref):
    s = _silu(c_ref[...]).astype(BF16)
    o_ref[...] = _dot(s, w_ref[...].astype(BF16)) + b_ref[...]


def _modulation(cond, w, b):
    tn = 1024
    n = w.shape[1]
    return pl.pallas_call(
        _mod_kernel,
        grid=(n // tn,),
        in_specs=[
            pl.BlockSpec((MOD_ROWS, D_MODEL), lambda j: (0, 0)),
            pl.BlockSpec((D_MODEL, tn), lambda j: (0, j)),
            pl.BlockSpec((1, tn), lambda j: (0, j)),
        ],
        out_specs=pl.BlockSpec((MOD_ROWS, tn), lambda j: (0, j)),
        out_shape=jax.ShapeDtypeStruct((MOD_ROWS, n), F32),
        compiler_params=_params("arbitrary"),
        name="modulation",
    )(cond, w, b.reshape(1, n))


def _mod_spec(which, tm, rows_per_mod, mod_off):
    return pl.BlockSpec(
        (None, None, 1, D_MODEL),
        lambda i, *_: (mod_off + (i * tm) // rows_per_mod, which, 0, 0))


def _norm_proj_kernel(x_ref, g_ref, sc_ref, sh_ref, w_ref, o_ref, h_scr):
    @pl.when(pl.program_id(1) == 0)
    def _():
        _norm_mod_rows(x_ref, g_ref, sc_ref, sh_ref, h_scr)

    o_ref[...] = _dot(h_scr[...], w_ref[...]).astype(o_ref.dtype)


def _norm_proj(x, g, mod4, sc_idx, sh_idx, w_tiles, rows_per_mod, mod_off, tm, out_dtype):
    m = x.shape[0]
    nt, _, tn = w_tiles.shape
    n = nt * tn
    return pl.pallas_call(
        _norm_proj_kernel,
        grid=(m // tm, n // tn),
        in_specs=[
            pl.BlockSpec((tm, D_MODEL), lambda i, j: (i, 0)),
            pl.BlockSpec((1, D_MODEL), lambda i, j: (0, 0)),
            _mod_spec(sc_idx, tm, rows_per_mod, mod_off),
            _mod_spec(sh_idx, tm, rows_per_mod, mod_off),
            pl.BlockSpec((None, D_MODEL, tn), lambda i, j: (j, 0, 0)),
        ],
        out_specs=pl.BlockSpec((tm, tn), lambda i, j: (i, j)),
        out_shape=jax.ShapeDtypeStruct((m, n), out_dtype),
        scratch_shapes=[pltpu.VMEM((tm, D_MODEL), BF16)],
        compiler_params=_params("parallel", "arbitrary"),
        name="norm_proj",
    )(x, g.reshape(1, D_MODEL), mod4, mod4, w_tiles)


def _gla_head(qa_ref, fzf_ref, fzb_ref, ia_ref, ga_ref, lbp_ref, gn_ref, s0_ref,
              out_ref, *rest, n, emit_state):
    if emit_state:
        snew_ref, ops_scr, dec_scr, p_scr, kv_scr, o_scr, st_scr = rest
    else:
        ops_scr, dec_scr, p_scr, kv_scr, o_scr, st_scr = rest
    c_len = SCAN_CHUNK
    nc = n // c_len
    mid = c_len // 2
    rb = GLA_ROW_BLOCK
    cpb = rb // c_len

    def lower_bound(d):
        a = [lbp_ref[d, l] for l in range(lbp_ref.shape[1])]
        mx = functools.reduce(jnp.maximum, a)
        e = [jnp.exp(t - mx) for t in a]
        return e[0] / functools.reduce(lambda u, w: u + w, e)

    lbs = (lower_bound(0), lower_bound(1))
    pos = lax.broadcasted_iota(jnp.int32, (rb, A_DK), 0) % c_len

    def operands(i):
        r0 = pl.multiple_of(i * rb, rb)
        rows = pl.ds(r0, rb)
        q = _silu(qa_ref[rows, :]) * (A_DK ** -0.5)
        for d, fz_ref in enumerate((fzf_ref, fzb_ref)):
            lb = lbs[d]
            f = lb + (1.0 - lb) * _sigmoid(fz_ref[rows, :])
            k = 1.0 - f
            b = jnp.log2(f)
            for s in (1, 2, 4, 8, 16, 32):
                if d == 0:
                    b = b + jnp.where(pos >= s, pltpu.roll(b, s, axis=0), 0.0)
                else:
                    b = b + jnp.where(pos < c_len - s, pltpu.roll(b, rb - s, axis=0), 0.0)
            for ci in range(cpb):
                sl = slice(ci * c_len, (ci + 1) * c_len)
                bc = b[sl]
                if d == 0:
                    ref, b_last = bc[mid - 1:mid], bc[c_len - 1:c_len]
                else:
                    ref, b_last = bc[c_len - mid:c_len - mid + 1], bc[0:1]
                qe = q[sl] * jnp.exp2(bc - ref)
                ke = k[sl] * jnp.exp2(ref - bc)
                crow = pl.ds(r0 + ci * c_len, c_len)
                ops_scr[d, 0, crow, :] = qe.astype(BF16)
                ops_scr[d, 1, crow, :] = ke.astype(BF16)
                ops_scr[d, 2, crow, :] = (qe * jnp.exp2(ref)).astype(BF16)
                ops_scr[d, 3, crow, :] = (ke * jnp.exp2(b_last - ref)).astype(BF16)
                dec_scr[d, i * cpb + ci] = jnp.broadcast_to(jnp.exp2(b_last), (8, A_DK))

    ri = lax.broadcasted_iota(jnp.int32, (c_len, c_len), 0)
    ci_ = lax.broadcasted_iota(jnp.int32, (c_len, c_len), 1)
    keep = (ri >= ci_, ri <= ci_)

    def chunk_products(c):
        rows = pl.ds(pl.multiple_of(c * c_len, c_len), c_len)
        vt = ia_ref[rows, :].T.astype(BF16)
        for d in range(2):
            scores = jnp.where(keep[d], _dot_nt(ops_scr[d, 0, rows, :], ops_scr[d, 1, rows, :]), 0.0)
            p_scr[d, rows, :] = scores.astype(BF16)
            kv_scr[d, c] = _dot(vt, ops_scr[d, 3, rows, :])

    def init_state():
        for d in range(2):
            st_scr[d] = s0_ref[d].T

    def scan_step(c):
        for d in range(2):
            cidx = c if d == 0 else nc - 1 - c
            rows = pl.ds(pl.multiple_of(cidx * c_len, c_len), c_len)
            st = st_scr[d]
            o_scr[d, rows, :] = (_dot(p_scr[d, rows, :], ia_ref[rows, :].astype(BF16))
                                 + _dot_nt(ops_scr[d, 2, rows, :], st.astype(BF16)))
            st_scr[d] = dec_scr[d, cidx][0:1, :] * st + kv_scr[d, cidx]

    def write_state():
        if emit_state:
            for d in range(2):
                snew_ref[d] = st_scr[d].T

    def finish(i):
        rows = pl.ds(pl.multiple_of(i * rb, rb), rb)
        o = _rms(o_scr[0, rows, :] + o_scr[1, rows, :], gn_ref[...]) * _silu(ga_ref[rows, :])
        out_ref[rows, :] = o.astype(out_ref.dtype)

    return operands, chunk_products, init_state, scan_step, write_state, finish


def _gla_kernel(qa_ref, fzf_ref, fzb_ref, ia_ref, ga_ref, lbp_ref, gn_ref, s0_ref, out_ref, *rest,
                n, emit_state, heads):
    snew_ref = rest[0] if emit_state else None
    scratch = rest[1:] if emit_state else rest
    phases = []
    for hh in range(heads):
        cols = pl.ds(hh * A_DK, A_DK)
        head_rest = tuple(r.at[hh] for r in scratch)
        if emit_state:
            head_rest = (snew_ref.at[:, hh],) + head_rest
        phases.append(_gla_head(
            qa_ref.at[:, cols], fzf_ref.at[:, cols], fzb_ref.at[:, cols], ia_ref.at[:, cols],
            ga_ref.at[:, cols], lbp_ref.at[:, :, hh], gn_ref, s0_ref.at[:, hh],
            out_ref.at[:, cols], *head_rest, n=n, emit_state=emit_state))
    operands, chunk_products, init_state, scan_step, write_state, finish = zip(*phases)
    nc = n // SCAN_CHUNK
    unroll = True if nc <= 4 else GLA_SCAN_UNROLL

    def each(fns):
        def body(i, carry):
            for fn in fns:
                fn(i)
            return carry
        return body

    lax.fori_loop(0, n // GLA_ROW_BLOCK, each(operands), 0)
    lax.fori_loop(0, nc, each(chunk_products), 0, unroll=unroll)
    for fn in init_state:
        fn()
    lax.fori_loop(0, nc, each(scan_step), 0, unroll=unroll)
    for fn in write_state:
        fn()
    lax.fori_loop(0, n // GLA_ROW_BLOCK, each(finish), 0)


def _gla(proj3, hgrn_lb, gnorm, s0, emit_state):
    bsz, n, _ = proj3.shape
    h = A_HEADS
    g = max(GLA_HEADS_PER_STEP, min(h, GLA_ROWS_PER_STEP // n))
    hg = h // g

    def col(k):
        return pl.BlockSpec((None, n, g * A_DK), lambda b, hh: (b, 0, k * hg + hh))

    n_lb = hgrn_lb.shape[1]
    lb_spec = pl.BlockSpec((2, n_lb, g, 1, A_DK), lambda b, hh: (0, 0, hh, 0, 0))
    st_spec = pl.BlockSpec((None, 2, g, A_DK, A_DV), lambda b, hh: (b, 0, hh, 0, 0))
    out_shape = [jax.ShapeDtypeStruct((bsz, n, A_WIDTH), BF16)]
    out_specs = [pl.BlockSpec((None, n, g * A_DV), lambda b, hh: (b, 0, hh))]
    if emit_state:
        out_shape.append(jax.ShapeDtypeStruct((bsz, 2, h, A_DK, A_DV), F32))
        out_specs.append(st_spec)
    res = pl.pallas_call(
        functools.partial(_gla_kernel, n=n, emit_state=emit_state, heads=g),
        grid=(bsz, hg),
        in_specs=[col(0), col(1), col(2), col(3), col(4), lb_spec,
                  pl.BlockSpec((1, A_DV), lambda b, hh: (0, 0)), st_spec],
        out_specs=out_specs,
        out_shape=out_shape,
        scratch_shapes=[pltpu.VMEM((g, 2, 4, n, A_DK), BF16),
                        pltpu.VMEM((g, 2, n // SCAN_CHUNK, 8, A_DK), F32),
                        pltpu.VMEM((g, 2, n, SCAN_CHUNK), BF16),
                        pltpu.VMEM((g, 2, n // SCAN_CHUNK, A_DV, A_DK), F32),
                        pltpu.VMEM((g, 2, n, A_DV), F32),
                        pltpu.VMEM((g, 2, A_DV, A_DK), F32)],
        compiler_params=_params("parallel", "parallel"),
        name="hgrn2",
    )(proj3, proj3, proj3, proj3, proj3, hgrn_lb.reshape(2, n_lb, h, 1, A_DK),
      gnorm.reshape(1, A_DV), s0)
    return res


def _gmlp_kernel(ub_ref, vb_ref, vn_ref, ws_ref, bst_ref, out_ref, *, rows):
    for g in range(B_GROUPS):
        cs = slice(g * B_CG, (g + 1) * B_CG)
        vv = _rms(_gelu_tanh(vb_ref[:, cs]), vn_ref[:, cs]).astype(BF16)
        w = ws_ref[g].astype(BF16)
        bias = bst_ref[:, g:g + 1]
        for c in range(rows // B_CHUNK):
            rs = slice(c * B_CHUNK, (c + 1) * B_CHUNK)
            mixed = _dot(w, vv[rs, :]) + bias
            out_ref[rs, cs] = (_gelu_tanh(ub_ref[rs, cs]) * mixed).astype(out_ref.dtype)


def _gmlp(proj, vnorm, ws, bs, rows):
    m = proj.shape[0]
    ub_blk = 5 * A_WIDTH // B_WIDTH
    return pl.pallas_call(
        functools.partial(_gmlp_kernel, rows=rows),
        grid=(m // rows,),
        in_specs=[
            pl.BlockSpec((rows, B_WIDTH), lambda i: (i, ub_blk)),
            pl.BlockSpec((rows, B_WIDTH), lambda i: (i, ub_blk + 1)),
            pl.BlockSpec((1, B_WIDTH), lambda i: (0, 0)),
            pl.BlockSpec((B_GROUPS, B_CHUNK, B_CHUNK), lambda i: (0, 0, 0)),
            pl.BlockSpec((B_CHUNK, B_GROUPS), lambda i: (0, 0)),
        ],
        out_specs=pl.BlockSpec((rows, B_WIDTH), lambda i: (i, 0)),
        out_shape=jax.ShapeDtypeStruct((m, B_WIDTH), BF16),
        compiler_params=_params("parallel"),
        name="gmlp",
    )(proj, proj, vnorm.reshape(1, B_WIDTH), ws, bs.T)


def _resid_proj_kernel(*refs, n_parts):
    x_ref, gate_ref = refs[0], refs[1]
    a_refs = refs[2:2 + n_parts]
    w_refs = refs[2 + n_parts:2 + 2 * n_parts]
    o_ref = refs[2 + 2 * n_parts]
    acc = _dot(a_refs[0][...], w_refs[0][...])
    for a_ref, w_ref in zip(a_refs[1:], w_refs[1:]):
        acc = acc + _dot(a_ref[...], w_ref[...])
    o_ref[...] = x_ref[...] + gate_ref[...] * acc


def _resid_proj(x, mod4, gate_idx, parts, w, rows_per_mod, mod_off, tm):
    m = x.shape[0]
    n_parts = len(parts)
    kp = parts[0].shape[1]
    in_specs = [pl.BlockSpec((tm, D_MODEL), lambda i: (i, 0)),
                _mod_spec(gate_idx, tm, rows_per_mod, mod_off)]
    in_specs += [pl.BlockSpec((tm, kp), lambda i: (i, 0)) for _ in parts]
    in_specs += [pl.BlockSpec((kp, D_MODEL), lambda i, p=p: (p, 0)) for p in range(n_parts)]
    return pl.pallas_call(
        functools.partial(_resid_proj_kernel, n_parts=n_parts),
        grid=(m // tm,),
        in_specs=in_specs,
        out_specs=pl.BlockSpec((tm, D_MODEL), lambda i: (i, 0)),
        out_shape=jax.ShapeDtypeStruct((m, D_MODEL), F32),
        compiler_params=_params("parallel"),
        name="resid_proj",
    )(x, mod4, *parts, *([w] * n_parts))


def _dft_tables(n):
    idx = np.arange(n, dtype=np.int64)
    ang = 2.0 * np.pi * ((idx[:, None] * idx[None, :]) % n).astype(np.float64) / n
    s = 1.0 / np.sqrt(n)
    return (np.cos(ang) * s).astype(np.float32), (np.sin(ang) * s).astype(np.float32)


def _norm_chan_dft_kernel(x_ref, g_ref, sc_ref, sh_ref, cc_ref, sn_ref, p_ref, q_ref, h_scr):
    _norm_mod_rows(x_ref, g_ref, sc_ref, sh_ref, h_scr)
    for g in range(C_GROUPS):
        cs = slice(g * C_CG, (g + 1) * C_CG)
        p_ref[:, cs] = _dot(h_scr[:, cs], cc_ref[...]).astype(p_ref.dtype)
        q_ref[:, cs] = _dot(h_scr[:, cs], sn_ref[...]).astype(q_ref.dtype)


def _norm_chan_dft(x, g, mod4, sc_idx, sh_idx, rows_per_mod, mod_off, tm):
    m = x.shape[0]
    cc, sn = _dft_tables(C_CG)
    cc = jnp.asarray(cc).astype(BF16)
    sn = jnp.asarray(sn).astype(BF16)
    tab = pl.BlockSpec((C_CG, C_CG), lambda i: (0, 0))
    out = pl.BlockSpec((tm, D_MODEL), lambda i: (i, 0))
    return pl.pallas_call(
        _norm_chan_dft_kernel,
        grid=(m // tm,),
        in_specs=[
            pl.BlockSpec((tm, D_MODEL), lambda i: (i, 0)),
            pl.BlockSpec((1, D_MODEL), lambda i: (0, 0)),
            _mod_spec(sc_idx, tm, rows_per_mod, mod_off),
            _mod_spec(sh_idx, tm, rows_per_mod, mod_off),
            tab, tab,
        ],
        out_specs=[out, out],
        out_shape=[jax.ShapeDtypeStruct((m, D_MODEL), BF16)] * 2,
        scratch_shapes=[pltpu.VMEM((tm, D_MODEL), BF16)],
        compiler_params=_params("parallel"),
        name="norm_chan_dft",
    )(x, g.reshape(1, D_MODEL), mod4, mod4, cc, sn)


def _pos_dft_kernel(cn_ref, sn_ref, p_ref, q_ref, o_ref):
    for b in range(o_ref.shape[0]):
        o_ref[b] = (_dot(cn_ref[...], p_ref[b]) - _dot(sn_ref[...], q_ref[b])).astype(o_ref.dtype)


def _pos_dft(p3, q3, tn):
    bsz, n, _ = p3.shape
    bb = max(1, min(bsz, POS_DFT_ROWS // n))
    cn, sn = _dft_tables(n)
    cn = jnp.asarray(cn).astype(BF16)
    sn = jnp.asarray(sn).astype(BF16)
    tab = pl.BlockSpec((n, n), lambda b, j: (0, 0))
    blk = pl.BlockSpec((bb, n, tn), lambda b, j: (b, 0, j))
    return pl.pallas_call(
        _pos_dft_kernel,
        grid=(bsz // bb, D_MODEL // tn),
        in_specs=[tab, tab, blk, blk],
        out_specs=blk,
        out_shape=jax.ShapeDtypeStruct((bsz, n, D_MODEL), BF16),
        compiler_params=_params("parallel", "parallel"),
        name="pos_dft",
    )(cn, sn, p3, q3)


FFN_SCHED = {name: i for i, name in enumerate((
    "up_rows", "up_mod", "up_col", "conv_col", "down_rows", "down_mod", "down_col",
    "norm_due", "reset_due", "out_due"))}


def _ffn_kernel(sched_ref, xc_ref, gf_ref, modu_ref, xp_ref, modd_ref, wgv_ref, cp_ref, wd_ref,
                mask_ref, o_ref, h_scr, acc_scr, u0_scr, u1_scr, act0_scr, act1_scr, *,
                final_norm, pieces):
    t = pl.program_id(0)
    tm, tf = act0_scr.shape
    lanes = mask_ref.shape[2]

    def due(name):
        return sched_ref[FFN_SCHED[name], t] != 0

    @pl.when(t == 0)
    def _():
        for ref in (u0_scr, u1_scr, act0_scr, act1_scr, acc_scr):
            ref[...] = jnp.zeros_like(ref)

    @pl.when(due("norm_due"))
    def _():
        _norm_mod_rows(xc_ref, gf_ref.at[0], modu_ref.at[1], modu_ref.at[0], h_scr)

    @pl.when(due("reset_due"))
    def _():
        acc_scr[...] = jnp.zeros_like(acc_scr)

    cb_rows = FFN_CONV_ROWS
    n_row_blocks = tm // cb_rows
    mask_rows = mask_ref.shape[1]

    def conv(u_old, k, r, cs):
        p0 = 4 * k
        ext = u_old[k, r * cb_rows:(r + 1) * cb_rows + 2 * HALO, cs]
        n_ext = cb_rows + 2 * HALO
        ms = slice((r * cb_rows) % mask_rows, (r * cb_rows) % mask_rows + cb_rows)
        inner = slice(HALO, HALO + cb_rows)
        prev = pltpu.roll(ext, 1, axis=0)[inner] * mask_ref[0, ms, :]
        nxt = pltpu.roll(ext, n_ext - 1, axis=0)[inner] * mask_ref[1, ms, :]
        return (prev * cp_ref[p0:p0 + 1, cs] + ext[inner] * cp_ref[p0 + 1:p0 + 2, cs]
                + nxt * cp_ref[p0 + 2:p0 + 3, cs] + cp_ref[p0 + 3:p0 + 4, cs])

    def conv_block(u_old, act_new, idx):
        c, r = divmod(idx, n_row_blocks)
        cs = slice(c * lanes, (c + 1) * lanes)
        gate = conv(u_old, 0, r, cs)
        val = conv(u_old, 1, r, cs)
        act_new[r * cb_rows:(r + 1) * cb_rows, cs] = (_silu(gate) * val).astype(BF16)

    def stages(u_new, u_old, act_new, act_old):
        n_blocks = (tf // lanes) * n_row_blocks
        up_pieces, down_pieces = pieces
        un = 2 * tf // up_pieces
        dn = D_MODEL // down_pieces
        per_half = tf // un
        h = h_scr[...]
        act = act_old[...]
        mxu = ([("up", q, un * D_MODEL) for q in range(up_pieces)]
               + [("down", q, dn * tf) for q in range(down_pieces)])
        early = n_blocks - FFN_TAIL_BLOCKS
        total = sum(w for _, _, w in mxu[:-1])
        done = 0
        spent = 0
        for n, (kind, q, w) in enumerate(mxu):
            if kind == "down":
                ns = slice(q * dn, (q + 1) * dn)
                acc_scr[:, ns] += _dot(act, wd_ref[:, ns])
            else:
                k, qq = divmod(q, per_half)
                ns = slice(qq * un, (qq + 1) * un)
                u_new[k, HALO:HALO + tm, ns] = _dot(h, wgv_ref[k, :, ns])
            spent += w
            target = n_blocks if n == len(mxu) - 1 else min(early, -(-early * spent // total))
            for idx in range(done, target):
                conv_block(u_old, act_new, idx)
            done = target

    @pl.when(t % 2 == 0)
    def _():
        stages(u0_scr, u1_scr, act1_scr, act0_scr)

    @pl.when(t % 2 == 1)
    def _():
        stages(u1_scr, u0_scr, act0_scr, act1_scr)

    @pl.when(due("out_due"))
    def _():
        rb = NORM_ROW_BLOCK

        def body(i, carry):
            rows = pl.ds(pl.multiple_of(i * rb, rb), rb)
            y = xp_ref[rows, :] + modd_ref[2] * acc_scr[rows, :]
            if final_norm:
                y = _rms(y, gf_ref[1])
            o_ref[rows, :] = y
            acc_scr[rows, :] = jnp.zeros((rb, D_MODEL), F32)
            return carry

        lax.fori_loop(0, tm // rb, body, 0, unroll=NORM_UNROLL)


def _ffn_conv_params(cw, cb, tf):
    cp = jnp.concatenate([cw[:, :D_FF], cb[None, :D_FF], cw[:, D_FF:], cb[None, D_FF:]], axis=0)
    return cp.reshape(8, D_FF // tf, tf).transpose(1, 0, 2)


def _conv_ffn(x, g, mod4, w_up, conv_p, w_down, fin, rows_per_mod, mod_off, row_len, tm,
              final_norm, pieces):
    m = x.shape[0]
    nf, _, tf = conv_p.shape
    n_tiles = (m // tm) * nf
    rb = max(row_len, FFN_CONV_ROWS)
    assert rb % row_len == 0 and rb % FFN_CONV_ROWS == 0 and tm % rb == 0
    pos = np.arange(rb) % row_len
    mask = np.stack([np.broadcast_to((pos != 0)[:, None], (rb, 128)),
                     np.broadcast_to((pos != row_len - 1)[:, None], (rb, 128))]).astype(np.float32)

    t = np.arange(n_tiles + 2)
    up = np.minimum(t, n_tiles - 1)
    cv = np.clip(t - 1, 0, n_tiles - 1)
    dn = np.clip(t - 2, 0, n_tiles - 1)
    sched = np.zeros((len(FFN_SCHED), n_tiles + 2), np.int32)
    sched[FFN_SCHED["up_rows"]] = up // nf
    sched[FFN_SCHED["up_mod"]] = mod_off + (up // nf * tm) // rows_per_mod
    sched[FFN_SCHED["up_col"]] = up % nf
    sched[FFN_SCHED["conv_col"]] = cv % nf
    sched[FFN_SCHED["down_rows"]] = dn // nf
    sched[FFN_SCHED["down_mod"]] = mod_off + (dn // nf * tm) // rows_per_mod
    sched[FFN_SCHED["down_col"]] = dn % nf
    sched[FFN_SCHED["norm_due"]] = (t < n_tiles) & (t % nf == 0)
    sched[FFN_SCHED["reset_due"]] = t == 2
    sched[FFN_SCHED["out_due"]] = (t >= 2) & ((t - 2) % nf == nf - 1)

    def row(name):
        return FFN_SCHED[name]

    def mod_spec(name):
        return pl.BlockSpec((None, 3, 1, D_MODEL), lambda t, s: (s[row(name), t], 1, 0, 0))

    grid_spec = pltpu.PrefetchScalarGridSpec(
        num_scalar_prefetch=1,
        grid=(n_tiles + 2,),
        in_specs=[
            pl.BlockSpec((tm, D_MODEL), lambda t, s: (s[row("up_rows"), t], 0)),
            pl.BlockSpec((2, 1, D_MODEL), lambda t, s: (0, 0, 0)),
            mod_spec("up_mod"),
            pl.BlockSpec((tm, D_MODEL), lambda t, s: (s[row("down_rows"), t], 0)),
            mod_spec("down_mod"),
            pl.BlockSpec((2, D_MODEL, tf), lambda t, s: (s[row("up_col"), t], 0, 0)),
            pl.BlockSpec((None, 8, tf), lambda t, s: (s[row("conv_col"), t], 0, 0)),
            pl.BlockSpec((tf, D_MODEL), lambda t, s: (s[row("down_col"), t], 0)),
            pl.BlockSpec((2, rb, 128), lambda t, s: (0, 0, 0)),
        ],
        out_specs=pl.BlockSpec((tm, D_MODEL), lambda t, s: (s[row("down_rows"), t], 0)),
        scratch_shapes=[pltpu.VMEM((tm, D_MODEL), BF16), pltpu.VMEM((tm, D_MODEL), F32),
                        pltpu.VMEM((2, tm + 2 * HALO, tf), F32),
                        pltpu.VMEM((2, tm + 2 * HALO, tf), F32),
                        pltpu.VMEM((tm, tf), BF16), pltpu.VMEM((tm, tf), BF16)],
    )
    return pl.pallas_call(
        functools.partial(_ffn_kernel, final_norm=final_norm, pieces=pieces),
        grid_spec=grid_spec,
        out_shape=jax.ShapeDtypeStruct((m, D_MODEL), F32),
        compiler_params=_params("arbitrary"),
        name="conv_ffn",
    )(jnp.asarray(sched), x, jnp.stack([g, fin]).reshape(2, 1, D_MODEL), mod4, x, mod4, w_up, conv_p,
      w_down, jnp.asarray(mask))


def kernel(x_prompt, x_sample, state_l0_hgrn, c, c_ctx, mod_w_0, mod_b_0, norm1_0, w_in_0, hgrn_lb, hgrn_gnorm_0, gmlp_vnorm_0, gmlp_ws_0, gmlp_bs_0, w_out_0, norm2_0, ffn_up_0, ffn_conv_w_0, ffn_conv_b_0, ffn_down_0, mod_w_1, mod_b_1, norm1_1, w_out_1, norm2_1, ffn_up_1, ffn_conv_w_1, ffn_conv_b_1, ffn_down_1, final_norm):
    n_dec = c.shape[0]
    cond = jnp.concatenate(
        [c, c_ctx[None, :], jnp.zeros((MOD_ROWS - n_dec - 1, D_MODEL), F32)], axis=0)
    mod4 = [_modulation(cond, w, b).reshape(MOD_ROWS, 6, 1, D_MODEL)
            for w, b in ((mod_w_0, mod_b_0), (mod_w_1, mod_b_1))]

    w_in = _cast_col_tiles(w_in_0, PROJ_COL_TILE)
    w_out0 = _cast_rows(w_out_0, CAST_ROWS)
    w_out1 = _cast_rows(w_out_1, CAST_ROWS)
    ffn = [(norm2_0, _cast_col_tiles(ffn_up_0, FFN_TF, pair_halves=True), _ffn_conv_params(ffn_conv_w_0, ffn_conv_b_0, FFN_TF),
            _cast_rows(ffn_down_0, CAST_ROWS)),
           (norm2_1, _cast_col_tiles(ffn_up_1, FFN_TF, pair_halves=True), _ffn_conv_params(ffn_conv_w_1, ffn_conv_b_1, FFN_TF),
            _cast_rows(ffn_down_1, CAST_ROWS))]

    def trunk(x3, s0, mod_off, per_batch_mod, row_len, emit_state):
        bsz, n, _ = x3.shape
        m = bsz * n
        rows_per_mod = n if per_batch_mod else m
        tm = ROW_TILE
        x = x3.reshape(m, D_MODEL)

        proj = _norm_proj(x, norm1_0, mod4[0], 1, 0, w_in, rows_per_mod, mod_off, PROJ_ROW_TILE, F32)
        res = _gla(proj.reshape(bsz, n, IN_WIDTH), hgrn_lb, hgrn_gnorm_0, s0,
                   emit_state)
        out_b = _gmlp(proj, gmlp_vnorm_0, gmlp_ws_0, gmlp_bs_0, GMLP_ROWS)
        x = _resid_proj(x, mod4[0], 2, [res[0].reshape(m, A_WIDTH), out_b], w_out0,
                        rows_per_mod, mod_off, tm)
        x = _conv_ffn(x, ffn[0][0], mod4[0], *ffn[0][1:], final_norm, rows_per_mod, mod_off,
                      row_len, tm, False, FFN_MXU_PIECES)

        p, q = _norm_chan_dft(x, norm1_1, mod4[1], 1, 0, rows_per_mod, mod_off, CHAN_DFT_ROW_TILE)
        four = _pos_dft(p.reshape(bsz, n, D_MODEL), q.reshape(bsz, n, D_MODEL), POS_DFT_COL_TILE)
        x = _resid_proj(x, mod4[1], 2, [four.reshape(m, D_MODEL)], w_out1, rows_per_mod, mod_off, tm)
        x = _conv_ffn(x, ffn[1][0], mod4[1], *ffn[1][1:], final_norm, rows_per_mod, mod_off,
                      row_len, tm, True, FFN_MXU_PIECES)
        return x.reshape(bsz, n, D_MODEL), (res[1] if emit_state else None)

    zero_state = jnp.zeros((x_prompt.shape[0], 2, A_HEADS, A_DK, A_DV), F32)
    y_prompt, state_new = trunk(x_prompt, zero_state, n_dec, False, x_prompt.shape[1], True)
    y_sample, _ = trunk(x_sample, state_l0_hgrn, 0, True, GRID_W, False)
    return (y_prompt, y_sample, state_new.astype(x_prompt.dtype))
```
